```python
import math, functools
import jax, jax.numpy as jnp
from jax import lax
import numpy as np

D_MODEL = 1024
BATCH = 1
SEQ = 16384
DEPTH = 4
DEC_BATCH = 32
DEC_SEQ = 32
PAST_LEN = 1024

CHUNK = 64
N_HEADS = 4
HEAD_DIM = 64
ATTN_WIDTH = N_HEADS * 2 * HEAD_DIM
GMLP_CHUNK = 128
GMLP_GROUPS = 4
GMLP_WIDTH = 512
GMLP_GROUP_DIM = GMLP_WIDTH // GMLP_GROUPS
ROPE_THETA = 10000.0
Q_BLOCK = 128
ALPHA = (2 * DEPTH) ** 0.25
BETA = (8 * DEPTH) ** -0.25
LN_EPS = 1e-5
SCALE = HEAD_DIM ** -0.5
IN_WIDTH = 4 * ATTN_WIDTH + 3 * GMLP_WIDTH + 2 * D_MODEL
SPLITS = (ATTN_WIDTH, 2 * ATTN_WIDTH, 3 * ATTN_WIDTH, 4 * ATTN_WIDTH,
          4 * ATTN_WIDTH + GMLP_WIDTH, 4 * ATTN_WIDTH + 2 * GMLP_WIDTH,
          4 * ATTN_WIDTH + 3 * GMLP_WIDTH, 4 * ATTN_WIDTH + 3 * GMLP_WIDTH + D_MODEL)

kernel_name = "diffattn_gmlp_gated_streaming_encoder"


def layernorm(x, g, b):
    xf = x.astype(jnp.float32)
    mu = jnp.mean(xf, -1, keepdims=True)
    var = jnp.mean(jnp.square(xf - mu), -1, keepdims=True)
    return ((xf - mu) * lax.rsqrt(var + LN_EPS) * g.astype(jnp.float32) + b.astype(jnp.float32)).astype(x.dtype)


def rmsnorm(x, g):
    xf = x.astype(jnp.float32)
    return (xf * lax.rsqrt(jnp.mean(jnp.square(xf), -1, keepdims=True) + LN_EPS) * g.astype(jnp.float32)).astype(x.dtype)


def rope(x, pos):
    half = HEAD_DIM // 2
    inv = ROPE_THETA ** (-jnp.arange(half, dtype=jnp.float32) / half)
    ang = pos.astype(jnp.float32)[:, None] * inv[None, :]
    cos = jnp.cos(ang)[None, :, None, :]
    sin = jnp.sin(ang)[None, :, None, :]
    x1 = x[..., :half].astype(jnp.float32)
    x2 = x[..., half:].astype(jnp.float32)
    return jnp.concatenate([x1 * cos - x2 * sin, x2 * cos + x1 * sin], -1).astype(x.dtype)


def diff_combine(s, lam, v):
    n, _, q, l = s.shape
    p = jax.nn.softmax(s, axis=-1).reshape(n, N_HEADS, 2, q, l)
    a = p[:, :, 0] - lam * p[:, :, 1]
    return jnp.einsum("nhqk,nkhe->nqhe", a.astype(v.dtype), v)


def diff_attn_prompt(q, k, v, lam):
    b, s = q.shape[:2]
    nb = s // Q_BLOCK
    qb = q.reshape(b, nb, Q_BLOCK, 2 * N_HEADS, HEAD_DIM).transpose(1, 0, 2, 3, 4)
    k_chunk = jnp.arange(s) // CHUNK

    def block(args):
        qi, i = args
        q_chunk = (i * Q_BLOCK + jnp.arange(Q_BLOCK)) // CHUNK
        mask = k_chunk[None, :] <= q_chunk[:, None]
        sc = jnp.einsum("bqhd,bkhd->bhqk", qi, k).astype(jnp.float32) * SCALE
        sc = jnp.where(mask[None, None], sc, -jnp.inf)
        return diff_combine(sc, lam, v)

    out = lax.map(block, (qb, jnp.arange(nb)))
    return out.transpose(1, 0, 2, 3, 4).reshape(b, s, N_HEADS, 2 * HEAD_DIM)


def diff_attn_sample(q, k, v, lam, ck, cv):
    kk = jnp.concatenate([ck.astype(k.dtype), k], axis=1)
    vv = jnp.concatenate([cv.astype(v.dtype), v], axis=1)
    sc = jnp.einsum("nqhd,nkhd->nhqk", q, kk).astype(jnp.float32) * SCALE
    return diff_combine(sc, lam, vv)


def sgu_prompt(vn, ws, bs):
    b, s, _ = vn.shape
    vc = vn.reshape(b, s // GMLP_CHUNK, GMLP_CHUNK, GMLP_GROUPS, GMLP_GROUP_DIM)
    out = jnp.einsum("gij,bcjgd->bcigd", ws.astype(vn.dtype), vc) + bs.T.astype(vn.dtype)[None, None, :, :, None]
    return out.reshape(b, s, GMLP_WIDTH)


def sgu_sample(vn, ws, bs):
    n, t, _ = vn.shape
    vc = vn.reshape(n, t, GMLP_GROUPS, GMLP_GROUP_DIM)
    out = jnp.einsum("gij,njgd->nigd", ws[:, :t, :t].astype(vn.dtype), vc) + bs[:, :t].T.astype(vn.dtype)[None, :, :, None]
    return out.reshape(n, t, GMLP_WIDTH)


def mixer_layer(x, pos, w_in, w_oa, w_og, w_out, lq1, lk1, lq2, lk2, subln_w,
                sgu_g, sgu_b, ln_g, ln_b, lam_init, attn_fn, sgu_fn):
    n, t, _ = x.shape
    h = x @ w_in
    q, k, va, ga, u, vg, gg, ma, mb = jnp.split(h, SPLITS, axis=-1)
    q = rope(q.reshape(n, t, 2 * N_HEADS, HEAD_DIM), pos)
    k = rope(k.reshape(n, t, 2 * N_HEADS, HEAD_DIM), pos)
    va = va.reshape(n, t, N_HEADS, 2 * HEAD_DIM)
    lam = (jnp.exp(jnp.sum(lq1.astype(jnp.float32) * lk1.astype(jnp.float32)))
           - jnp.exp(jnp.sum(lq2.astype(jnp.float32) * lk2.astype(jnp.float32))) + lam_init)
    o = attn_fn(q, k, va, lam)
    o = rmsnorm(o, subln_w) * (1.0 - lam_init)
    y_a = (o.reshape(n, t, ATTN_WIDTH) * jax.nn.silu(ga)) @ w_oa
    vn = layernorm(vg, sgu_g, sgu_b)
    y_g = (u * sgu_fn(vn) * jax.nn.silu(gg)) @ w_og
    merged = jax.nn.sigmoid(ma) * y_a + jax.nn.sigmoid(mb) * y_g
    x_new = layernorm(ALPHA * x + merged @ w_out, ln_g, ln_b)
    return x_new, k, va, vn


def setup_inputs(seed: int = 0) -> dict:
    key = jax.random.key(seed)
    ks = jax.random.split(key, 20)
    f32 = jnp.float32
    nrm = lambda k, shape: jax.random.normal(k, shape, dtype=f32)
    col_scale = jnp.ones((IN_WIDTH,), f32).at[2 * ATTN_WIDTH:3 * ATTN_WIDTH].set(BETA)
    return {
        "x_prompt": nrm(ks[0], (BATCH, SEQ, D_MODEL)),
        "x_sample": nrm(ks[1], (DEC_BATCH, DEC_SEQ, D_MODEL)),
        "cache_k": nrm(ks[2], (DEPTH, DEC_BATCH, PAST_LEN, 2 * N_HEADS, HEAD_DIM)),
        "cache_v": nrm(ks[3], (DEPTH, DEC_BATCH, PAST_LEN, N_HEADS, 2 * HEAD_DIM)) * BETA,
        "w_in": nrm(ks[4], (DEPTH, D_MODEL, IN_WIDTH)) * D_MODEL ** -0.5 * col_scale,
        "w_oa": nrm(ks[5], (DEPTH, ATTN_WIDTH, D_MODEL)) * ATTN_WIDTH ** -0.5 * BETA,
        "w_og": nrm(ks[6], (DEPTH, GMLP_WIDTH, D_MODEL)) * GMLP_WIDTH ** -0.5 * BETA,
        "w_out": nrm(ks[7], (DEPTH, D_MODEL, D_MODEL)) * D_MODEL ** -0.5 * BETA,
        "lambda_q1": nrm(ks[8], (DEPTH, HEAD_DIM)) * 0.1,
        "lambda_k1": nrm(ks[9], (DEPTH, HEAD_DIM)) * 0.1,
        "lambda_q2": nrm(ks[10], (DEPTH, HEAD_DIM)) * 0.1,
        "lambda_k2": nrm(ks[11], (DEPTH, HEAD_DIM)) * 0.1,
        "subln_w": 1.0 + 0.02 * nrm(ks[12], (DEPTH, 2 * HEAD_DIM)),
        "sgu_ln_g": 1.0 + 0.02 * nrm(ks[13], (DEPTH, GMLP_WIDTH)),
        "sgu_ln_b": 0.02 * nrm(ks[14], (DEPTH, GMLP_WIDTH)),
        "w_s": nrm(ks[15], (DEPTH, GMLP_GROUPS, GMLP_CHUNK, GMLP_CHUNK)) * GMLP_CHUNK ** -0.5,
        "b_s": 1.0 + 0.01 * nrm(ks[16], (DEPTH, GMLP_GROUPS, GMLP_CHUNK)),
        "ln_g": 1.0 + 0.02 * nrm(ks[17], (DEPTH, D_MODEL)),
        "ln_b": 0.02 * nrm(ks[18], (DEPTH, D_MODEL)),
    }


def reference(x_prompt, x_sample, cache_k, cache_v, w_in, w_oa, w_og, w_out,
              lambda_q1, lambda_k1, lambda_q2, lambda_k2, subln_w, sgu_ln_g, sgu_ln_b,
              w_s, b_s, ln_g, ln_b):
    pos_p = jnp.arange(x_prompt.shape[1])
    pos_s = PAST_LEN + jnp.arange(x_sample.shape[1])
    tril = jnp.tril(jnp.ones((GMLP_CHUNK, GMLP_CHUNK), dtype=bool))
    xp, xs = x_prompt, x_sample
    kp_l, vp_l, ks_l, vs_l, gs_l = [], [], [], [], []
    for l in range(DEPTH):
        lam_init = 0.8 - 0.6 * math.exp(-0.3 * l)
        ws_m = jnp.where(tril[None], w_s[l], 0.0)
        shared = (w_in[l], w_oa[l], w_og[l], w_out[l], lambda_q1[l], lambda_k1[l],
                  lambda_q2[l], lambda_k2[l], subln_w[l], sgu_ln_g[l], sgu_ln_b[l],
                  ln_g[l], ln_b[l], lam_init)
        xp, kp, vp, _ = mixer_layer(
            xp, pos_p, *shared, diff_attn_prompt,
            functools.partial(sgu_prompt, ws=ws_m, bs=b_s[l]))
        xs, ksn, vsn, gsn = mixer_layer(
            xs, pos_s, *shared,
            functools.partial(diff_attn_sample, ck=cache_k[l], cv=cache_v[l]),
            functools.partial(sgu_sample, ws=ws_m, bs=b_s[l]))
        kp_l.append(kp); vp_l.append(vp)
        ks_l.append(ksn); vs_l.append(vsn); gs_l.append(gsn)
    new_k_prompt = jnp.stack(kp_l)
    new_v_prompt = jnp.stack(vp_l)
    new_k_sample = jnp.stack(ks_l)
    new_v_sample = jnp.stack(vs_l)
    new_gv_sample = jnp.stack(gs_l)
    return (xp, xs, new_k_prompt, new_v_prompt, new_k_sample, new_v_sample, new_gv_sample)
```

```python
import functools
import math

import jax
import jax.numpy as jnp
import numpy as np
from jax import lax
from jax.experimental import pallas as pl
from jax.experimental.pallas import tpu as pltpu

D_MODEL = 1024
DEPTH = 4
PAST_LEN = 1024
CHUNK = 64
N_HEADS = 4
HEAD_DIM = 64
VALUE_DIM = 2 * HEAD_DIM
ATTN_WIDTH = N_HEADS * VALUE_DIM
GMLP_CHUNK = 128
GMLP_GROUPS = 4
GMLP_WIDTH = 512
ROPE_THETA = 10000.0
ALPHA = (2 * DEPTH) ** 0.25
LN_EPS = 1e-5
SCALE = HEAD_DIM ** -0.5
QKV_WIDTH = 3 * ATTN_WIDTH
REST_WIDTH = ATTN_WIDTH + 3 * GMLP_WIDTH + 2 * D_MODEL

LANES = 128
VMEM_LIMIT_BYTES = 56 * 1024 * 1024

ROW_TILE = 512
ATTN_TILE = 512

F32 = jnp.float32
BF16 = jnp.bfloat16


def _nt_dot(a, b):
    return lax.dot_general(a, b, (((1,), (1,)), ((), ())), preferred_element_type=F32)


def _layernorm(x, g, b):
    mu = jnp.mean(x, axis=-1, keepdims=True)
    xc = x - mu
    var = jnp.mean(xc * xc, axis=-1, keepdims=True)
    return xc * lax.rsqrt(var + LN_EPS) * g + b


def _silu(x):
    return x * jax.nn.sigmoid(x)


def _qkv_kernel(x_ref, w_ref, cos_ref, sin_ref, qb_ref, kb_ref, vb_ref, k_ref, v_ref):
    xb = x_ref[...].astype(BF16)
    h = jnp.dot(xb, w_ref[...], preferred_element_type=F32)
    cos = cos_ref[...]
    sin = sin_ref[...]
    lane = lax.broadcasted_iota(jnp.int32, cos.shape, 1)
    first_half = (lane % HEAD_DIM) < (HEAD_DIM // 2)

    def rope(z):
        partner = jnp.where(first_half, pltpu.roll(z, LANES - HEAD_DIM // 2, 1),
                            pltpu.roll(z, HEAD_DIM // 2, 1))
        return z * cos + partner * sin

    for c in range(ATTN_WIDTH // LANES):
        sl = slice(c * LANES, (c + 1) * LANES)
        q = rope(h[:, c * LANES:(c + 1) * LANES])
        qb_ref[:, sl] = (q * SCALE).astype(BF16)
        k = rope(h[:, ATTN_WIDTH + c * LANES:ATTN_WIDTH + (c + 1) * LANES])
        k_ref[:, sl] = k
        kb_ref[:, sl] = k.astype(BF16)
    v = h[:, 2 * ATTN_WIDTH:3 * ATTN_WIDTH]
    v_ref[...] = v
    vb_ref[...] = v.astype(BF16)


def _qkv_call(x, w_qkv, cos_t, sin_t):
    n = x.shape[0]
    tm = ROW_TILE
    row = lambda i: (i, 0)
    full = lambda i: (0, 0)
    return pl.pallas_call(
        _qkv_kernel,
        grid=(n // tm,),
        in_specs=[
            pl.BlockSpec((tm, D_MODEL), row),
            pl.BlockSpec((D_MODEL, QKV_WIDTH), full),
            pl.BlockSpec((tm, LANES), row),
            pl.BlockSpec((tm, LANES), row),
        ],
        out_specs=[pl.BlockSpec((tm, ATTN_WIDTH), row)] * 5,
        out_shape=[jax.ShapeDtypeStruct((n, ATTN_WIDTH), BF16)] * 3
        + [jax.ShapeDtypeStruct((n, ATTN_WIDTH), F32)] * 2,
        compiler_params=pltpu.CompilerParams(
            dimension_semantics=("parallel",), vmem_limit_bytes=VMEM_LIMIT_BYTES),
        name="qkv_rope",
    )(x, w_qkv, cos_t, sin_t)


def _lambda_value(lq1_ref, lk1_ref, lq2_ref, lk2_ref, lam_init):
    a = jnp.sum(lq1_ref[...] * lk1_ref[...], axis=-1, keepdims=True)
    b = jnp.sum(lq2_ref[...] * lk2_ref[...], axis=-1, keepdims=True)
    return jnp.exp(a) - jnp.exp(b) + lam_init


def _stack_sub_queries(q_ref, qs_ref, rows):
    lane = lax.broadcasted_iota(jnp.int32, (rows, LANES), 1)
    low = lane < HEAD_DIM
    zero = jnp.zeros((rows, LANES), BF16)
    for h in range(N_HEADS):
        qh = q_ref[:, h * LANES:(h + 1) * LANES]
        qs_ref[h, :rows, :] = jnp.where(low, qh, zero)
        qs_ref[h, rows:, :] = jnp.where(low, zero, qh)


def _combine_heads(acc0, acc1, l0, l1, lam, subw, lam_init):
    o = acc0 / l0 - lam * (acc1 / l1)
    ms = jnp.mean(o * o, axis=-1, keepdims=True)
    return o * lax.rsqrt(ms + LN_EPS) * subw * (1.0 - lam_init)


def _attn_prompt_kernel(qi_ref, ki_ref, q_ref, k_ref, v_ref, lq1_ref, lk1_ref, lq2_ref,
                        lk2_ref, subw_ref, o_ref, qs_ref, m_ref, l_ref, acc_ref, *,
                        tile, lam_init):
    step = pl.program_id(0)
    qi = qi_ref[step]
    ki = ki_ref[step]

    @pl.when(ki == 0)
    def _():
        _stack_sub_queries(q_ref, qs_ref, tile)
        m_ref[...] = jnp.full(m_ref.shape, -jnp.inf, F32)
        l_ref[...] = jnp.zeros(l_ref.shape, F32)
        acc_ref[...] = jnp.zeros(acc_ref.shape, F32)

    def update(masked):
        if masked:
            r = lax.broadcasted_iota(jnp.int32, (2 * tile, tile), 0)
            c = lax.broadcasted_iota(jnp.int32, (2 * tile, tile), 1)
            visible = (c // CHUNK) <= ((r % tile) // CHUNK)
        for h in range(N_HEADS):
            hs = slice(h * LANES, (h + 1) * LANES)
            s = _nt_dot(qs_ref[h], k_ref[:, hs])
            if masked:
                s = jnp.where(visible, s, -jnp.inf)
            m_prev = m_ref[h]
            m_next = jnp.maximum(m_prev, jnp.max(s, axis=-1, keepdims=True))
            alpha = jnp.exp(m_prev - m_next)
            p = jnp.exp(s - pltpu.repeat(m_next, tile // LANES, axis=1))
            psum = p[:, :LANES]
            for cblk in range(1, tile // LANES):
                psum = psum + p[:, cblk * LANES:(cblk + 1) * LANES]
            l_ref[h] = alpha * l_ref[h] + psum
            m_ref[h] = m_next
            acc_ref[h] = alpha * acc_ref[h] + jnp.dot(
                p.astype(BF16), v_ref[:, hs], preferred_element_type=F32)

    @pl.when(ki < qi)
    def _():
        update(False)

    @pl.when(ki == qi)
    def _():
        update(True)
        lam = _lambda_value(lq1_ref, lk1_ref, lq2_ref, lk2_ref, lam_init)
        subw = subw_ref[...]
        for h in range(N_HEADS):
            l = jnp.sum(l_ref[h], axis=-1, keepdims=True)
            acc = acc_ref[h]
            o_ref[:, h * LANES:(h + 1) * LANES] = _combine_heads(
                acc[:tile], acc[tile:], l[:tile], l[tile:], lam, subw, lam_init)


def _attn_prompt_call(qb, kb, vb, lq1, lk1, lq2, lk2, subw, lam_init):
    s = qb.shape[0]
    t = ATTN_TILE
    nq = s // t
    qi = np.concatenate([np.full(i + 1, i, np.int32) for i in range(nq)])
    ki = np.concatenate([np.arange(i + 1, dtype=np.int32) for i in range(nq)])
    vec = pl.BlockSpec((1, HEAD_DIM), lambda g, qi, ki: (0, 0))
    grid_spec = pltpu.PrefetchScalarGridSpec(
        num_scalar_prefetch=2,
        grid=(len(qi),),
        in_specs=[
            pl.BlockSpec((t, ATTN_WIDTH), lambda g, qi, ki: (qi[g], 0)),
            pl.BlockSpec((t, ATTN_WIDTH), lambda g, qi, ki: (ki[g], 0)),
            pl.BlockSpec((t, ATTN_WIDTH), lambda g, qi, ki: (ki[g], 0)),
            vec, vec, vec, vec,
            pl.BlockSpec((1, VALUE_DIM), lambda g, qi, ki: (0, 0)),
        ],
        out_specs=pl.BlockSpec((t, ATTN_WIDTH), lambda g, qi, ki: (qi[g], 0)),
        scratch_shapes=[
            pltpu.VMEM((N_HEADS, 2 * t, LANES), BF16),
            pltpu.VMEM((N_HEADS, 2 * t, LANES), F32),
            pltpu.VMEM((N_HEADS, 2 * t, LANES), F32),
            pltpu.VMEM((N_HEADS, 2 * t, VALUE_DIM), F32),
        ],
    )
    return pl.pallas_call(
        functools.partial(_attn_prompt_kernel, tile=t, lam_init=lam_init),
        grid_spec=grid_spec,
        out_shape=jax.ShapeDtypeStruct((s, ATTN_WIDTH), F32),
        compiler_params=pltpu.CompilerParams(
            dimension_semantics=("arbitrary",), vmem_limit_bytes=VMEM_LIMIT_BYTES),
        name="diff_attn_prompt",
    )(jnp.asarray(qi), jnp.asarray(ki), qb, kb, vb, lq1, lk1, lq2, lk2, subw)


def _attn_sample_kernel(q_ref, k_ref, v_ref, ck_ref, cv_ref, lq1_ref, lk1_ref, lq2_ref,
                        lk2_ref, subw_ref, o_ref, qs_ref, *, rows, lam_init):
    _stack_sub_queries(q_ref, qs_ref, rows)
    lam = _lambda_value(lq1_ref, lk1_ref, lq2_ref, lk2_ref, lam_init)
    subw = subw_ref[...]
    for h in range(N_HEADS):
        hs = slice(h * LANES, (h + 1) * LANES)
        qs = qs_ref[h]
        s_c = _nt_dot(qs, ck_ref[:, hs].astype(BF16))
        s_n = _nt_dot(qs, k_ref[:, hs])
        m = jnp.maximum(jnp.max(s_c, axis=-1, keepdims=True),
                        jnp.max(s_n, axis=-1, keepdims=True))
        p_c = jnp.exp(s_c - m)
        p_n = jnp.exp(s_n - m)
        l = jnp.sum(p_c, axis=-1, keepdims=True) + jnp.sum(p_n, axis=-1, keepdims=True)
        acc = jnp.dot(p_c.astype(BF16), cv_ref[:, hs].astype(BF16), preferred_element_type=F32)
        acc = acc + jnp.dot(p_n.astype(BF16), v_ref[:, hs], preferred_element_type=F32)
        o_ref[:, hs] = _combine_heads(acc[:rows], acc[rows:], l[:rows], l[rows:], lam, subw,
                                      lam_init)


def _attn_sample_call(qb, kb, vb, ck, cv, lq1, lk1, lq2, lk2, subw, lam_init, rows):
    n = qb.shape[0] // rows
    past = ck.shape[1]
    vec = pl.BlockSpec((1, HEAD_DIM), lambda i: (0, 0))
    new = pl.BlockSpec((rows, ATTN_WIDTH), lambda i: (i, 0))
    cache = pl.BlockSpec((None, past, ATTN_WIDTH), lambda i: (i, 0, 0))
    return pl.pallas_call(
        functools.partial(_attn_sample_kernel, rows=rows, lam_init=lam_init),
        grid=(n,),
        in_specs=[new, new, new, cache, cache, vec, vec, vec, vec,
                  pl.BlockSpec((1, VALUE_DIM), lambda i: (0, 0))],
        out_specs=new,
        out_shape=jax.ShapeDtypeStruct((n * rows, ATTN_WIDTH), F32),
        scratch_shapes=[pltpu.VMEM((N_HEADS, 2 * rows, LANES), BF16)],
        compiler_params=pltpu.CompilerParams(
            dimension_semantics=("parallel",), vmem_limit_bytes=VMEM_LIMIT_BYTES),
        name="diff_attn_sample",
    )(qb, kb, vb, ck, cv, lq1, lk1, lq2, lk2, subw)


_GA, _U, _VG, _GG, _MA, _MB = (
    (0, ATTN_WIDTH),
    (ATTN_WIDTH, ATTN_WIDTH + GMLP_WIDTH),
    (ATTN_WIDTH + GMLP_WIDTH, ATTN_WIDTH + 2 * GMLP_WIDTH),
    (ATTN_WIDTH + 2 * GMLP_WIDTH, ATTN_WIDTH + 3 * GMLP_WIDTH),
    (ATTN_WIDTH + 3 * GMLP_WIDTH, ATTN_WIDTH + 3 * GMLP_WIDTH + D_MODEL),
    (ATTN_WIDTH + 3 * GMLP_WIDTH + D_MODEL, REST_WIDTH),
)


def _post_kernel(x_ref, o_ref, w_ref, woa_ref, wog_ref, wout_ref, sg_ref, sb_ref, ws_ref,
                 bs_ref, lng_ref, lnb_ref, *out_and_scratch, period, emit_vn):
    if emit_vn:
        xo_ref, vn_ref, sgu_ref = out_and_scratch
    else:
        xo_ref, sgu_ref = out_and_scratch
    tm = x_ref.shape[0]
    x = x_ref[...]
    xb = x.astype(BF16)

    def proj(cols):
        return jnp.dot(xb, w_ref[:, cols[0]:cols[1]], preferred_element_type=F32)

    ta = (o_ref[...] * _silu(proj(_GA))).astype(BF16)
    ya = jnp.dot(ta, woa_ref[...], preferred_element_type=F32)

    vn = _layernorm(proj(_VG), sg_ref[...], sb_ref[...])
    if emit_vn:
        vn_ref[...] = vn
    vnb = vn.astype(BF16)
    r = lax.broadcasted_iota(jnp.int32, (GMLP_CHUNK, GMLP_CHUNK), 0)
    c = lax.broadcasted_iota(jnp.int32, (GMLP_CHUNK, GMLP_CHUNK), 1)
    causal = (c <= r) & ((r // period) == (c // period))
    bias = bs_ref[...]
    for g in range(GMLP_GROUPS):
        gs = slice(g * LANES, (g + 1) * LANES)
        wsg = jnp.where(causal, ws_ref[g], 0.0).astype(BF16)
        for ch in range(tm // GMLP_CHUNK):
            rs = slice(ch * GMLP_CHUNK, (ch + 1) * GMLP_CHUNK)
            sgu_ref[rs, gs] = jnp.dot(wsg, vnb[rs, gs], preferred_element_type=F32) + bias[:, gs]
    tg = (proj(_U) * sgu_ref[...] * _silu(proj(_GG))).astype(BF16)
    yg = jnp.dot(tg, wog_ref[...], preferred_element_type=F32)

    merged = jax.nn.sigmoid(proj(_MA)) * ya + jax.nn.sigmoid(proj(_MB)) * yg
    y = jnp.dot(merged.astype(BF16), wout_ref[...], preferred_element_type=F32)
    xo_ref[...] = _layernorm(ALPHA * x + y, lng_ref[...], lnb_ref[...])


def _post_call(x, o, w_rest, w_oa, w_og, w_out, sg, sb, ws, bs, lng, lnb, period, emit_vn):
    n = x.shape[0]
    tm = ROW_TILE
    row = lambda i: (i, 0)
    full2 = lambda i: (0, 0)
    out_specs = [pl.BlockSpec((tm, D_MODEL), row)]
    out_shape = [jax.ShapeDtypeStruct((n, D_MODEL), F32)]
    if emit_vn:
        out_specs.append(pl.BlockSpec((tm, GMLP_WIDTH), row))
        out_shape.append(jax.ShapeDtypeStruct((n, GMLP_WIDTH), F32))
    return pl.pallas_call(
        functools.partial(_post_kernel, period=period, emit_vn=emit_vn),
        grid=(n // tm,),
        in_specs=[
            pl.BlockSpec((tm, D_MODEL), row),
            pl.BlockSpec((tm, ATTN_WIDTH), row),
            pl.BlockSpec((D_MODEL, REST_WIDTH), full2),
            pl.BlockSpec((ATTN_WIDTH, D_MODEL), full2),
            pl.BlockSpec((GMLP_WIDTH, D_MODEL), full2),
            pl.BlockSpec((D_MODEL, D_MODEL), full2),
            pl.BlockSpec((1, GMLP_WIDTH), full2),
            pl.BlockSpec((1, GMLP_WIDTH), full2),
            pl.BlockSpec((GMLP_GROUPS, GMLP_CHUNK, GMLP_CHUNK), lambda i: (0, 0, 0)),
            pl.BlockSpec((GMLP_CHUNK, GMLP_WIDTH), full2),
            pl.BlockSpec((1, D_MODEL), full2),
            pl.BlockSpec((1, D_MODEL), full2),
        ],
        out_specs=out_specs,
        out_shape=out_shape,
        scratch_shapes=[pltpu.VMEM((tm, GMLP_WIDTH), F32)],
        compiler_params=pltpu.CompilerParams(
            dimension_semantics=("parallel",), vmem_limit_bytes=VMEM_LIMIT_BYTES),
        name="post_mix",
    )(x, o, w_rest, w_oa, w_og, w_out, sg, sb, ws, bs, lng, lnb)


def _rope_tables(pos):
    half = HEAD_DIM // 2
    inv = ROPE_THETA ** (-jnp.arange(half, dtype=F32) / half)
    ang = pos.astype(F32)[:, None] * inv[None, :]
    cos = jnp.cos(ang)
    sin = jnp.sin(ang)
    reps = LANES // HEAD_DIM
    cos_t = jnp.tile(jnp.concatenate([cos, cos], -1), (1, reps))
    sin_t = jnp.tile(jnp.concatenate([-sin, sin], -1), (1, reps))
    return cos_t, sin_t


def kernel(x_prompt, x_sample, cache_k, cache_v, w_in, w_oa, w_og, w_out, lambda_q1,
           lambda_k1, lambda_q2, lambda_k2, subln_w, sgu_ln_g, sgu_ln_b, w_s, b_s, ln_g, ln_b):
    batch, seq, _ = x_prompt.shape
    dec_batch, dec_seq, _ = x_sample.shape
    assert batch == 1 and seq % ATTN_TILE == 0 and seq % ROW_TILE == 0
    assert (dec_batch * dec_seq) % ROW_TILE == 0 and GMLP_CHUNK % dec_seq == 0

    cos_p, sin_p = _rope_tables(jnp.arange(seq))
    cos_s, sin_s = _rope_tables(PAST_LEN + jnp.arange(dec_seq))
    cos_s = jnp.tile(cos_s, (dec_batch, 1))
    sin_s = jnp.tile(sin_s, (dec_batch, 1))

    xp = x_prompt.reshape(seq, D_MODEL)
    xs = x_sample.reshape(dec_batch * dec_seq, D_MODEL)
    ck_all = cache_k.reshape(DEPTH, dec_batch, PAST_LEN, ATTN_WIDTH)
    cv_all = cache_v.reshape(DEPTH, dec_batch, PAST_LEN, ATTN_WIDTH)

    kp_l, vp_l, ks_l, vs_l, gs_l = [], [], [], [], []
    for l in range(DEPTH):
        lam_init = 0.8 - 0.6 * math.exp(-0.3 * l)
        w_qkv = w_in[l][:, :QKV_WIDTH].astype(BF16)
        w_rest = w_in[l][:, QKV_WIDTH:].astype(BF16)
        woa = w_oa[l].astype(BF16)
        wog = w_og[l].astype(BF16)
        wout = w_out[l].astype(BF16)
        lam_vecs = tuple(v[l].reshape(1, HEAD_DIM)
                         for v in (lambda_q1, lambda_k1, lambda_q2, lambda_k2))
        subw = subln_w[l].reshape(1, VALUE_DIM)
        sg = sgu_ln_g[l].reshape(1, GMLP_WIDTH)
        sb = sgu_ln_b[l].reshape(1, GMLP_WIDTH)
        lng = ln_g[l].reshape(1, D_MODEL)
        lnb = ln_b[l].reshape(1, D_MODEL)
        bias_p = jnp.repeat(b_s[l].T, GMLP_WIDTH // GMLP_GROUPS, axis=1)
        reps = GMLP_CHUNK // dec_seq
        ws_s = jnp.tile(w_s[l][:, :dec_seq, :dec_seq], (1, reps, reps))
        bias_s = jnp.tile(bias_p[:dec_seq], (reps, 1))

        qb, kb, vb, kp, vp = _qkv_call(xp, w_qkv, cos_p, sin_p)
        op = _attn_prompt_call(qb, kb, vb, *lam_vecs, subw, lam_init)
        (xp,) = _post_call(xp, op, w_rest, woa, wog, wout, sg, sb, w_s[l], bias_p, lng, lnb,
                           GMLP_CHUNK, False)

        qb, kb, vb, ksn, vsn = _qkv_call(xs, w_qkv, cos_s, sin_s)
        osn = _attn_sample_call(qb, kb, vb, ck_all[l], cv_all[l], *lam_vecs, subw, lam_init,
                                dec_seq)
        xs, gsn = _post_call(xs, osn, w_rest, woa, wog, wout, sg, sb, ws_s, bias_s, lng, lnb,
                             dec_seq, True)

        kp_l.append(kp); vp_l.append(vp)
        ks_l.append(ksn); vs_l.append(vsn); gs_l.append(gsn)

    new_k_prompt = jnp.stack(kp_l).reshape(DEPTH, batch, seq, 2 * N_HEADS, HEAD_DIM)
    new_v_prompt = jnp.stack(vp_l).reshape(DEPTH, batch, seq, N_HEADS, VALUE_DIM)
    new_k_sample = jnp.stack(ks_l).reshape(DEPTH, dec_batch, dec_seq, 2 * N_HEADS, HEAD_DIM)
    new_v_sample = jnp.stack(vs_l).reshape(DEPTH, dec_batch, dec_seq, N_HEADS, VALUE_DIM)
    new_gv_sample = jnp.stack(gs_l).reshape(DEPTH, dec_batch, dec_seq, GMLP_WIDTH)
    return (xp.reshape(batch, seq, D_MODEL), xs.reshape(dec_batch, dec_seq, D_MODEL),
            new_k_prompt, new_v_prompt, new_k_sample, new_v_sample, new_gv_sample)
```

```python
import functools
import math

import jax
import jax.numpy as jnp
import numpy as np
from jax import lax
from jax.experimental import pallas as pl
from jax.experimental.pallas import tpu as pltpu

D_MODEL = 1024
DEPTH = 4
PAST_LEN = 1024
CHUNK = 64
N_HEADS = 4
HEAD_DIM = 64
HALF_DIM = HEAD_DIM // 2
VALUE_DIM = 2 * HEAD_DIM
ATTN_WIDTH = N_HEADS * VALUE_DIM
GMLP_CHUNK = 128
GMLP_GROUPS = 4
GMLP_WIDTH = 512
ROPE_THETA = 10000.0
ALPHA = (2 * DEPTH) ** 0.25
LN_EPS = 1e-5
SCALE = HEAD_DIM ** -0.5
QKV_WIDTH = 3 * ATTN_WIDTH
IN_WIDTH = QKV_WIDTH + ATTN_WIDTH + 3 * GMLP_WIDTH + 2 * D_MODEL

LANES = 128
VMEM_LIMIT_BYTES = 56 * 1024 * 1024

ROW_TILE = 512
ATTN_TILE = 512

F32 = jnp.float32
BF16 = jnp.bfloat16


def _nt_dot(a, b):
    return lax.dot_general(a, b, (((1,), (1,)), ((), ())), preferred_element_type=F32)


def _dot(a, b):
    return jnp.dot(a, b, preferred_element_type=F32)


def _layernorm(x, g, b):
    mu = jnp.mean(x, axis=-1, keepdims=True)
    xc = x - mu
    var = jnp.mean(xc * xc, axis=-1, keepdims=True)
    return xc * lax.rsqrt(var + LN_EPS) * g + b


def _silu(x):
    return x * jax.nn.sigmoid(x)


def _params(semantics):
    return pltpu.CompilerParams(dimension_semantics=semantics,
                                vmem_limit_bytes=VMEM_LIMIT_BYTES)


def _rope_lanes(z, cos, sin):
    lane = lax.broadcasted_iota(jnp.int32, z.shape, 1)
    first_half = (lane % HEAD_DIM) < HALF_DIM
    partner = jnp.where(first_half, pltpu.roll(z, LANES - HALF_DIM, 1),
                        pltpu.roll(z, HALF_DIM, 1))
    return z * cos + partner * sin


def _qkv_prompt_kernel(x_ref, wq_ref, wkt_ref, wv_ref, cos_ref, sin_ref, cost_ref, sint_ref,
                       kbuf_ref, vbuf_ref, qb_ref, ktb_ref, vb_ref, kt_ref, v_ref):
    del kbuf_ref, vbuf_ref
    xb = x_ref[...].astype(BF16)
    cos = cos_ref[...]
    sin = sin_ref[...]
    q = _dot(xb, wq_ref[...])
    for c in range(ATTN_WIDTH // LANES):
        sl = slice(c * LANES, (c + 1) * LANES)
        qb_ref[:, sl] = (_rope_lanes(q[:, sl], cos, sin) * SCALE).astype(BF16)

    kt = _nt_dot(wkt_ref[...], xb)
    cost = cost_ref[...]
    sint = sint_ref[...]
    for j in range(2 * N_HEADS):
        x1 = kt[j * HEAD_DIM:j * HEAD_DIM + HALF_DIM]
        x2 = kt[j * HEAD_DIM + HALF_DIM:(j + 1) * HEAD_DIM]
        o1 = x1 * cost - x2 * sint
        o2 = x2 * cost + x1 * sint
        kt_ref[j, :HALF_DIM, :] = o1
        kt_ref[j, HALF_DIM:, :] = o2
        ktb_ref[j * HEAD_DIM:j * HEAD_DIM + HALF_DIM, :] = o1.astype(BF16)
        ktb_ref[j * HEAD_DIM + HALF_DIM:(j + 1) * HEAD_DIM, :] = o2.astype(BF16)

    v = _dot(xb, wv_ref[...])
    vb_ref[...] = v.astype(BF16)
    for h in range(N_HEADS):
        v_ref[:, h, :] = v[:, h * LANES:(h + 1) * LANES]


def _qkv_prompt_call(l, x, w_in_b, wkt_b, cos_t, sin_t, cost_t, sint_t, kbuf, vbuf):
    n = x.shape[0]
    tm = ROW_TILE
    row = lambda i: (i, 0)
    col = lambda i: (0, i)
    any_spec = pl.BlockSpec(memory_space=pl.ANY)
    return pl.pallas_call(
        _qkv_prompt_kernel,
        grid=(n // tm,),
        in_specs=[
            pl.BlockSpec((tm, D_MODEL), row),
            pl.BlockSpec((None, D_MODEL, ATTN_WIDTH), lambda i: (l, 0, 0)),
            pl.BlockSpec((None, ATTN_WIDTH, D_MODEL), lambda i: (l, 0, 0)),
            pl.BlockSpec((None, D_MODEL, ATTN_WIDTH), lambda i: (l, 0, 2)),
            pl.BlockSpec((tm, LANES), row),
            pl.BlockSpec((tm, LANES), row),
            pl.BlockSpec((HALF_DIM, tm), col),
            pl.BlockSpec((HALF_DIM, tm), col),
            any_spec, any_spec,
        ],
        out_specs=[
            pl.BlockSpec((tm, ATTN_WIDTH), row),
            pl.BlockSpec((ATTN_WIDTH, tm), col),
            pl.BlockSpec((tm, ATTN_WIDTH), row),
            pl.BlockSpec((None, 2 * N_HEADS, HEAD_DIM, tm), lambda i: (l, 0, 0, i)),
            pl.BlockSpec((None, tm, N_HEADS, VALUE_DIM), lambda i: (l, i, 0, 0)),
        ],
        out_shape=[
            jax.ShapeDtypeStruct((n, ATTN_WIDTH), BF16),
            jax.ShapeDtypeStruct((ATTN_WIDTH, n), BF16),
            jax.ShapeDtypeStruct((n, ATTN_WIDTH), BF16),
            jax.ShapeDtypeStruct(kbuf.shape, F32),
            jax.ShapeDtypeStruct(vbuf.shape, F32),
        ],
        input_output_aliases={8: 3, 9: 4},
        compiler_params=_params(("parallel",)),
        name="qkv_rope_prompt",
    )(x, w_in_b, wkt_b, w_in_b, cos_t, sin_t, cost_t, sint_t, kbuf, vbuf)


def _qkv_sample_kernel(x_ref, w_ref, cos_ref, sin_ref, qb_ref, kb_ref, vb_ref, k_ref, v_ref):
    xb = x_ref[...].astype(BF16)
    h = _dot(xb, w_ref[...])
    cos = cos_ref[...]
    sin = sin_ref[...]
    for c in range(ATTN_WIDTH // LANES):
        sl = slice(c * LANES, (c + 1) * LANES)
        q = _rope_lanes(h[:, c * LANES:(c + 1) * LANES], cos, sin)
        qb_ref[:, sl] = (q * SCALE).astype(BF16)
        k = _rope_lanes(h[:, ATTN_WIDTH + c * LANES:ATTN_WIDTH + (c + 1) * LANES], cos, sin)
        k_ref[:, sl] = k
        kb_ref[:, sl] = k.astype(BF16)
    v = h[:, 2 * ATTN_WIDTH:3 * ATTN_WIDTH]
    v_ref[...] = v
    vb_ref[...] = v.astype(BF16)


def _qkv_sample_call(l, x, w_in_b, cos_t, sin_t):
    n = x.shape[0]
    tm = ROW_TILE
    row = lambda i: (i, 0)
    return pl.pallas_call(
        _qkv_sample_kernel,
        grid=(n // tm,),
        in_specs=[
            pl.BlockSpec((tm, D_MODEL), row),
            pl.BlockSpec((None, D_MODEL, QKV_WIDTH), lambda i: (l, 0, 0)),
            pl.BlockSpec((tm, LANES), row),
            pl.BlockSpec((tm, LANES), row),
        ],
        out_specs=[pl.BlockSpec((tm, ATTN_WIDTH), row)] * 5,
        out_shape=[jax.ShapeDtypeStruct((n, ATTN_WIDTH), BF16)] * 3
        + [jax.ShapeDtypeStruct((n, ATTN_WIDTH), F32)] * 2,
        compiler_params=_params(("parallel",)),
        name="qkv_rope_sample",
    )(x, w_in_b, cos_t, sin_t)


def _lambda_value(lq1_ref, lk1_ref, lq2_ref, lk2_ref, lam_init):
    a = jnp.sum(lq1_ref[...] * lk1_ref[...], axis=-1, keepdims=True)
    b = jnp.sum(lq2_ref[...] * lk2_ref[...], axis=-1, keepdims=True)
    return jnp.exp(a) - jnp.exp(b) + lam_init


def _stack_sub_queries(q_ref, qs_ref, rows):
    lane = lax.broadcasted_iota(jnp.int32, (rows, LANES), 1)
    low = lane < HEAD_DIM
    zero = jnp.zeros((rows, LANES), BF16)
    for h in range(N_HEADS):
        qh = q_ref[:, h * LANES:(h + 1) * LANES]
        qs_ref[h, :rows, :] = jnp.where(low, qh, zero)
        qs_ref[h, rows:, :] = jnp.where(low, zero, qh)


def _combine_heads(acc0, acc1, l0, l1, lam, subw, lam_init):
    o = acc0 / l0 - lam * (acc1 / l1)
    ms = jnp.mean(o * o, axis=-1, keepdims=True)
    return o * lax.rsqrt(ms + LN_EPS) * subw * (1.0 - lam_init)


def _attn_prompt_kernel(qi_ref, ki_ref, q_ref, kt_ref, v_ref, lq1_ref, lk1_ref, lq2_ref,
                        lk2_ref, subw_ref, o_ref, qs_ref, m_ref, l_ref, acc_ref, *,
                        tile, lam_init):
    step = pl.program_id(0)
    qi = qi_ref[step]
    ki = ki_ref[step]

    @pl.when(ki == 0)
    def _():
        _stack_sub_queries(q_ref, qs_ref, tile)
        m_ref[...] = jnp.full(m_ref.shape, -jnp.inf, F32)
        l_ref[...] = jnp.zeros(l_ref.shape, F32)
        acc_ref[...] = jnp.zeros(acc_ref.shape, F32)

    def update(masked):
        if masked:
            r = lax.broadcasted_iota(jnp.int32, (2 * tile, tile), 0)
            c = lax.broadcasted_iota(jnp.int32, (2 * tile, tile), 1)
            visible = (c // CHUNK) <= ((r % tile) // CHUNK)
        for h in range(N_HEADS):
            hs = slice(h * LANES, (h + 1) * LANES)
            s = _dot(qs_ref[h], kt_ref[hs, :])
            if masked:
                s = jnp.where(visible, s, -jnp.inf)
            m_prev = m_ref[h]
            m_next = jnp.maximum(m_prev, jnp.max(s, axis=-1, keepdims=True))
            alpha = jnp.exp(m_prev - m_next)
            p = jnp.exp(s - pltpu.repeat(m_next, tile // LANES, axis=1))
            psum = p[:, :LANES]
            for cblk in range(1, tile // LANES):
                psum = psum + p[:, cblk * LANES:(cblk + 1) * LANES]
            l_ref[h] = alpha * l_ref[h] + psum
            m_ref[h] = m_next
            acc_ref[h] = alpha * acc_ref[h] + _dot(p.astype(BF16), v_ref[:, hs])

    @pl.when(ki < qi)
    def _():
        update(False)

    @pl.when(ki == qi)
    def _():
        update(True)
        lam = _lambda_value(lq1_ref, lk1_ref, lq2_ref, lk2_ref, lam_init)
        subw = subw_ref[...]
        for h in range(N_HEADS):
            l = jnp.sum(l_ref[h], axis=-1, keepdims=True)
            acc = acc_ref[h]
            o_ref[:, h * LANES:(h + 1) * LANES] = _combine_heads(
                acc[:tile], acc[tile:], l[:tile], l[tile:], lam, subw, lam_init)


def _attn_prompt_call(l, qb, ktb, vb, lam_vecs, subw, lam_init):
    s = qb.shape[0]
    t = ATTN_TILE
    nq = s // t
    qi = np.concatenate([np.full(i + 1, i, np.int32) for i in range(nq)])
    ki = np.concatenate([np.arange(i + 1, dtype=np.int32) for i in range(nq)])
    vec = pl.BlockSpec((None, 1, HEAD_DIM), lambda g, qi, ki: (l, 0, 0))
    grid_spec = pltpu.PrefetchScalarGridSpec(
        num_scalar_prefetch=2,
        grid=(len(qi),),
        in_specs=[
            pl.BlockSpec((t, ATTN_WIDTH), lambda g, qi, ki: (qi[g], 0)),
            pl.BlockSpec((ATTN_WIDTH, t), lambda g, qi, ki: (0, ki[g])),
            pl.BlockSpec((t, ATTN_WIDTH), lambda g, qi, ki: (ki[g], 0)),
            vec, vec, vec, vec,
            pl.BlockSpec((None, 1, VALUE_DIM), lambda g, qi, ki: (l, 0, 0)),
        ],
        out_specs=pl.BlockSpec((t, ATTN_WIDTH), lambda g, qi, ki: (qi[g], 0)),
        scratch_shapes=[
            pltpu.VMEM((N_HEADS, 2 * t, LANES), BF16),
            pltpu.VMEM((N_HEADS, 2 * t, LANES), F32),
            pltpu.VMEM((N_HEADS, 2 * t, LANES), F32),
            pltpu.VMEM((N_HEADS, 2 * t, VALUE_DIM), F32),
        ],
    )
    return pl.pallas_call(
        functools.partial(_attn_prompt_kernel, tile=t, lam_init=lam_init),
        grid_spec=grid_spec,
        out_shape=jax.ShapeDtypeStruct((s, ATTN_WIDTH), F32),
        compiler_params=_params(("arbitrary",)),
        name="diff_attn_prompt",
    )(jnp.asarray(qi), jnp.asarray(ki), qb, ktb, vb, *lam_vecs, subw)


def _attn_sample_kernel(q_ref, k_ref, v_ref, ckt_ref, cv_ref, lq1_ref, lk1_ref, lq2_ref,
                        lk2_ref, subw_ref, o_ref, qs_ref, *, rows, lam_init):
    _stack_sub_queries(q_ref, qs_ref, rows)
    lam = _lambda_value(lq1_ref, lk1_ref, lq2_ref, lk2_ref, lam_init)
    subw = subw_ref[...]
    past = ckt_ref.shape[-1]
    for h in range(N_HEADS):
        hs = slice(h * LANES, (h + 1) * LANES)
        qs = qs_ref[h]
        ckt = ckt_ref[2 * h:2 * h + 2].reshape(LANES, past).astype(BF16)
        s_c = _dot(qs, ckt)
        s_n = _nt_dot(qs, k_ref[:, hs])
        m = jnp.maximum(jnp.max(s_c, axis=-1, keepdims=True),
                        jnp.max(s_n, axis=-1, keepdims=True))
        p_c = jnp.exp(s_c - m)
        p_n = jnp.exp(s_n - m)
        l = jnp.sum(p_c, axis=-1, keepdims=True) + jnp.sum(p_n, axis=-1, keepdims=True)
        acc = _dot(p_c.astype(BF16), cv_ref[:, h, :].astype(BF16))
        acc = acc + _dot(p_n.astype(BF16), v_ref[:, hs])
        o_ref[:, hs] = _combine_heads(acc[:rows], acc[rows:], l[:rows], l[rows:], lam, subw,
                                      lam_init)


def _attn_sample_call(l, qb, kb, vb, ckt, cv, lam_vecs, subw, lam_init, rows):
    n = qb.shape[0] // rows
    past = cv.shape[2]
    vec = pl.BlockSpec((None, 1, HEAD_DIM), lambda i: (l, 0, 0))
    new = pl.BlockSpec((rows, ATTN_WIDTH), lambda i: (i, 0))
    return pl.pallas_call(
        functools.partial(_attn_sample_kernel, rows=rows, lam_init=lam_init),
        grid=(n,),
        in_specs=[
            new, new, new,
            pl.BlockSpec((None, None, 2 * N_HEADS, HEAD_DIM, past), lambda i: (l, i, 0, 0, 0)),
            pl.BlockSpec((None, None, past, N_HEADS, VALUE_DIM), lambda i: (l, i, 0, 0, 0)),
            vec, vec, vec, vec,
            pl.BlockSpec((None, 1, VALUE_DIM), lambda i: (l, 0, 0)),
        ],
        out_specs=new,
        out_shape=jax.ShapeDtypeStruct((n * rows, ATTN_WIDTH), F32),
        scratch_shapes=[pltpu.VMEM((N_HEADS, 2 * rows, LANES), BF16)],
        compiler_params=_params(("parallel",)),
        name="diff_attn_sample",
    )(qb, kb, vb, ckt, cv, *lam_vecs, subw)


_GA = (QKV_WIDTH, QKV_WIDTH + ATTN_WIDTH)
_U = (_GA[1], _GA[1] + GMLP_WIDTH)
_VG = (_U[1], _U[1] + GMLP_WIDTH)
_GG = (_VG[1], _VG[1] + GMLP_WIDTH)
_MA = (_GG[1], _GG[1] + D_MODEL)
_MB = (_MA[1], IN_WIDTH)


def _post_kernel(x_ref, o_ref, w_ref, woa_ref, wog_ref, wout_ref, sg_ref, sb_ref, ws_ref,
                 bs_ref, lng_ref, lnb_ref, *out_and_scratch, period, emit_vn):
    if emit_vn:
        xo_ref, vn_ref, sgu_ref = out_and_scratch
    else:
        xo_ref, sgu_ref = out_and_scratch
    tm = x_ref.shape[0]
    x = x_ref[...]
    xb = x.astype(BF16)

    def proj(cols):
        return _dot(xb, w_ref[:, cols[0]:cols[1]])

    ta = (o_ref[...] * _silu(proj(_GA))).astype(BF16)
    ya = _dot(ta, woa_ref[...])

    vn = _layernorm(proj(_VG), sg_ref[...], sb_ref[...])
    if emit_vn:
        vn_ref[...] = vn
    vnb = vn.astype(BF16)
    r = lax.broadcasted_iota(jnp.int32, (GMLP_CHUNK, GMLP_CHUNK), 0)
    c = lax.broadcasted_iota(jnp.int32, (GMLP_CHUNK, GMLP_CHUNK), 1)
    causal = (c <= r) & ((r // period) == (c // period))
    bias = bs_ref[...]
    for g in range(GMLP_GROUPS):
        gs = slice(g * LANES, (g + 1) * LANES)
        wsg = jnp.where(causal, ws_ref[g], 0.0).astype(BF16)
        for ch in range(tm // GMLP_CHUNK):
            rs = slice(ch * GMLP_CHUNK, (ch + 1) * GMLP_CHUNK)
            sgu_ref[rs, gs] = _dot(wsg, vnb[rs, gs]) + bias[:, gs]
    tg = (proj(_U) * sgu_ref[...] * _silu(proj(_GG))).astype(BF16)
    yg = _dot(tg, wog_ref[...])

    merged = jax.nn.sigmoid(proj(_MA)) * ya + jax.nn.sigmoid(proj(_MB)) * yg
    y = _dot(merged.astype(BF16), wout_ref[...])
    xo_ref[...] = _layernorm(ALPHA * x + y, lng_ref[...], lnb_ref[...])


def _post_call(l, x, o, w_in_b, woa_b, wog_b, wout_b, sg, sb, ws, bs, lng, lnb, period,
               emit_vn):
    n = x.shape[0]
    tm = ROW_TILE
    row = lambda i: (i, 0)
    layer3 = lambda i: (l, 0, 0)
    out_specs = [pl.BlockSpec((tm, D_MODEL), row)]
    out_shape = [jax.ShapeDtypeStruct((n, D_MODEL), F32)]
    if emit_vn:
        out_specs.append(pl.BlockSpec((tm, GMLP_WIDTH), row))
        out_shape.append(jax.ShapeDtypeStruct((n, GMLP_WIDTH), F32))
    return pl.pallas_call(
        functools.partial(_post_kernel, period=period, emit_vn=emit_vn),
        grid=(n // tm,),
        in_specs=[
            pl.BlockSpec((tm, D_MODEL), row),
            pl.BlockSpec((tm, ATTN_WIDTH), row),
            pl.BlockSpec((None, D_MODEL, IN_WIDTH), layer3),
            pl.BlockSpec((None, ATTN_WIDTH, D_MODEL), layer3),
            pl.BlockSpec((None, GMLP_WIDTH, D_MODEL), layer3),
            pl.BlockSpec((None, D_MODEL, D_MODEL), layer3),
            pl.BlockSpec((None, 1, GMLP_WIDTH), layer3),
            pl.BlockSpec((None, 1, GMLP_WIDTH), layer3),
            pl.BlockSpec((None, GMLP_GROUPS, GMLP_CHUNK, GMLP_CHUNK), lambda i: (l, 0, 0, 0)),
            pl.BlockSpec((None, GMLP_CHUNK, GMLP_WIDTH), layer3),
            pl.BlockSpec((None, 1, D_MODEL), layer3),
            pl.BlockSpec((None, 1, D_MODEL), layer3),
        ],
        out_specs=out_specs,
        out_shape=out_shape,
        scratch_shapes=[pltpu.VMEM((tm, GMLP_WIDTH), F32)],
        compiler_params=_params(("parallel",)),
        name="post_mix",
    )(x, o, w_in_b, woa_b, wog_b, wout_b, sg, sb, ws, bs, lng, lnb)


def _rope_angles(pos):
    inv = ROPE_THETA ** (-jnp.arange(HALF_DIM, dtype=F32) / HALF_DIM)
    ang = pos.astype(F32)[:, None] * inv[None, :]
    return jnp.cos(ang), jnp.sin(ang)


def _lane_tables(cos, sin):
    reps = LANES // HEAD_DIM
    cos_t = jnp.tile(jnp.concatenate([cos, cos], -1), (1, reps))
    sin_t = jnp.tile(jnp.concatenate([-sin, sin], -1), (1, reps))
    return cos_t, sin_t


def kernel(x_prompt, x_sample, cache_k, cache_v, w_in, w_oa, w_og, w_out, lambda_q1,
           lambda_k1, lambda_q2, lambda_k2, subln_w, sgu_ln_g, sgu_ln_b, w_s, b_s, ln_g, ln_b):
    batch, seq, _ = x_prompt.shape
    dec_batch, dec_seq, _ = x_sample.shape
    assert batch == 1 and seq % ATTN_TILE == 0 and seq % ROW_TILE == 0
    assert (dec_batch * dec_seq) % ROW_TILE == 0 and GMLP_CHUNK % dec_seq == 0

    cos_p, sin_p = _rope_angles(jnp.arange(seq))
    cos_pl, sin_pl = _lane_tables(cos_p, sin_p)
    cos_pt, sin_pt = cos_p.T, sin_p.T
    cos_s, sin_s = _lane_tables(*_rope_angles(PAST_LEN + jnp.arange(dec_seq)))
    cos_s = jnp.tile(cos_s, (dec_batch, 1))
    sin_s = jnp.tile(sin_s, (dec_batch, 1))

    xp = x_prompt.reshape(seq, D_MODEL)
    xs = x_sample.reshape(dec_batch * dec_seq, D_MODEL)
    ckt = jnp.transpose(cache_k, (0, 1, 3, 4, 2))

    w_in_b = w_in.astype(BF16)
    wkt_b = jnp.transpose(w_in[:, :, ATTN_WIDTH:2 * ATTN_WIDTH], (0, 2, 1)).astype(BF16)
    woa_b = w_oa.astype(BF16)
    wog_b = w_og.astype(BF16)
    wout_b = w_out.astype(BF16)
    lam_vecs = tuple(v.reshape(DEPTH, 1, HEAD_DIM)
                     for v in (lambda_q1, lambda_k1, lambda_q2, lambda_k2))
    subw = subln_w.reshape(DEPTH, 1, VALUE_DIM)
    sg = sgu_ln_g.reshape(DEPTH, 1, GMLP_WIDTH)
    sb = sgu_ln_b.reshape(DEPTH, 1, GMLP_WIDTH)
    lng = ln_g.reshape(DEPTH, 1, D_MODEL)
    lnb = ln_b.reshape(DEPTH, 1, D_MODEL)
    bias_p = jnp.repeat(jnp.transpose(b_s, (0, 2, 1)), GMLP_WIDTH // GMLP_GROUPS, axis=2)
    reps = GMLP_CHUNK // dec_seq
    ws_s = jnp.tile(w_s[:, :, :dec_seq, :dec_seq], (1, 1, reps, reps))
    bias_s = jnp.tile(bias_p[:, :dec_seq], (1, reps, 1))

    kbuf = jnp.zeros((DEPTH, 2 * N_HEADS, HEAD_DIM, seq), F32)
    vbuf = jnp.zeros((DEPTH, seq, N_HEADS, VALUE_DIM), F32)

    ks_l, vs_l, gs_l = [], [], []
    for l in range(DEPTH):
        lam_init = 0.8 - 0.6 * math.exp(-0.3 * l)

        qb, ktb, vb, kbuf, vbuf = _qkv_prompt_call(l, xp, w_in_b, wkt_b, cos_pl, sin_pl,
                                                   cos_pt, sin_pt, kbuf, vbuf)
        op = _attn_prompt_call(l, qb, ktb, vb, lam_vecs, subw, lam_init)
        (xp,) = _post_call(l, xp, op, w_in_b, woa_b, wog_b, wout_b, sg, sb, w_s, bias_p,
                           lng, lnb, GMLP_CHUNK, False)

        qb, kb, vb, ksn, vsn = _qkv_sample_call(l, xs, w_in_b, cos_s, sin_s)
        osn = _attn_sample_call(l, qb, kb, vb, ckt, cache_v, lam_vecs, subw, lam_init, dec_seq)
        xs, gsn = _post_call(l, xs, osn, w_in_b, woa_b, wog_b, wout_b, sg, sb, ws_s, bias_s,
                             lng, lnb, dec_seq, True)
        ks_l.append(ksn); vs_l.append(vsn); gs_l.append(gsn)

    new_k_prompt = jnp.transpose(kbuf.reshape(DEPTH, batch, 2 * N_HEADS, HEAD_DIM, seq),
                                 (0, 1, 4, 2, 3))
    new_v_prompt = vbuf.reshape(DEPTH, batch, seq, N_HEADS, VALUE_DIM)
    new_k_sample = jnp.stack(ks_l).reshape(DEPTH, dec_batch, dec_seq, 2 * N_HEADS, HEAD_DIM)
    new_v_sample = jnp.stack(vs_l).reshape(DEPTH, dec_batch, dec_seq, N_HEADS, VALUE_DIM)
    new_gv_sample = jnp.stack(gs_l).reshape(DEPTH, dec_batch, dec_seq, GMLP_WIDTH)
    return (xp.reshape(batch, seq, D_MODEL), xs.reshape(dec_batch, dec_seq, D_MODEL),
            new_k_prompt, new_v_prompt, new_k_sample, new_v_sample, new_gv_sample)
```

```python
import functools
import math

import jax
import jax.numpy as jnp
import numpy as np
from jax import lax
from jax.experimental import pallas as pl
from jax.experimental.pallas import tpu as pltpu

D_MODEL = 1024
DEPTH = 4
PAST_LEN = 1024
CHUNK = 64
N_HEADS = 4
HEAD_DIM = 64
HALF_DIM = HEAD_DIM // 2
VALUE_DIM = 2 * HEAD_DIM
ATTN_WIDTH = N_HEADS * VALUE_DIM
GMLP_CHUNK = 128
GMLP_GROUPS = 4
GMLP_WIDTH = 512
ROPE_THETA = 10000.0
ALPHA = (2 * DEPTH) ** 0.25
LN_EPS = 1e-5
SCALE = HEAD_DIM ** -0.5
QKV_WIDTH = 3 * ATTN_WIDTH
IN_WIDTH = QKV_WIDTH + ATTN_WIDTH + 3 * GMLP_WIDTH + 2 * D_MODEL

LANES = 128
VMEM_LIMIT_BYTES = 56 * 1024 * 1024

ROW_TILE = 512
ATTN_TILE = 1024
ATTN_CHAIN = 512
VEXT_WIDTH = 2 * VALUE_DIM

F32 = jnp.float32
BF16 = jnp.bfloat16


def _nt_dot(a, b):
    return lax.dot_general(a, b, (((1,), (1,)), ((), ())), preferred_element_type=F32)


def _dot(a, b):
    return jnp.dot(a, b, preferred_element_type=F32)


def _layernorm(x, g, b):
    mu = jnp.mean(x, axis=-1, keepdims=True)
    xc = x - mu
    var = jnp.mean(xc * xc, axis=-1, keepdims=True)
    return xc * lax.rsqrt(var + LN_EPS) * g + b


def _silu(x):
    return x * jax.nn.sigmoid(x)


def _params(semantics):
    return pltpu.CompilerParams(dimension_semantics=semantics,
                                vmem_limit_bytes=VMEM_LIMIT_BYTES)


def _rope_lanes(z, cos, sin):
    lane = lax.broadcasted_iota(jnp.int32, z.shape, 1)
    first_half = (lane % HEAD_DIM) < HALF_DIM
    partner = jnp.where(first_half, pltpu.roll(z, LANES - HALF_DIM, 1),
                        pltpu.roll(z, HALF_DIM, 1))
    return z * cos + partner * sin


def _qkv_prompt_kernel(x_ref, wq_ref, wkt_ref, wv_ref, cos_ref, sin_ref, cost_ref, sint_ref,
                       kbuf_ref, vbuf_ref, qb_ref, ktb_ref, vb_ref, kt_ref, v_ref):
    del kbuf_ref, vbuf_ref
    xb = x_ref[...].astype(BF16)
    cos = cos_ref[...]
    sin = sin_ref[...]
    q = _dot(xb, wq_ref[...])
    for c in range(ATTN_WIDTH // LANES):
        sl = slice(c * LANES, (c + 1) * LANES)
        qb_ref[:, sl] = (_rope_lanes(q[:, sl], cos, sin) * SCALE).astype(BF16)

    kt = _nt_dot(wkt_ref[...], xb)
    cost = cost_ref[...]
    sint = sint_ref[...]
    for j in range(2 * N_HEADS):
        x1 = kt[j * HEAD_DIM:j * HEAD_DIM + HALF_DIM]
        x2 = kt[j * HEAD_DIM + HALF_DIM:(j + 1) * HEAD_DIM]
        o1 = x1 * cost - x2 * sint
        o2 = x2 * cost + x1 * sint
        kt_ref[j, :HALF_DIM, :] = o1
        kt_ref[j, HALF_DIM:, :] = o2
        ktb_ref[j * HEAD_DIM:j * HEAD_DIM + HALF_DIM, :] = o1.astype(BF16)
        ktb_ref[j * HEAD_DIM + HALF_DIM:(j + 1) * HEAD_DIM, :] = o2.astype(BF16)

    v = _dot(xb, wv_ref[...])
    ones = jnp.ones((v.shape[0], VALUE_DIM), BF16)
    for h in range(N_HEADS):
        vh = v[:, h * LANES:(h + 1) * LANES]
        v_ref[:, h, :] = vh
        vb_ref[:, h * VEXT_WIDTH:h * VEXT_WIDTH + VALUE_DIM] = vh.astype(BF16)
        vb_ref[:, h * VEXT_WIDTH + VALUE_DIM:(h + 1) * VEXT_WIDTH] = ones


def _qkv_prompt_call(l, x, w_in_b, wkt_b, cos_t, sin_t, cost_t, sint_t, kbuf, vbuf):
    n = x.shape[0]
    tm = ROW_TILE
    row = lambda i: (i, 0)
    col = lambda i: (0, i)
    any_spec = pl.BlockSpec(memory_space=pl.ANY)
    return pl.pallas_call(
        _qkv_prompt_kernel,
        grid=(n // tm,),
        in_specs=[
            pl.BlockSpec((tm, D_MODEL), row),
            pl.BlockSpec((None, D_MODEL, ATTN_WIDTH), lambda i: (l, 0, 0)),
            pl.BlockSpec((None, ATTN_WIDTH, D_MODEL), lambda i: (l, 0, 0)),
            pl.BlockSpec((None, D_MODEL, ATTN_WIDTH), lambda i: (l, 0, 2)),
            pl.BlockSpec((tm, LANES), row),
            pl.BlockSpec((tm, LANES), row),
            pl.BlockSpec((HALF_DIM, tm), col),
            pl.BlockSpec((HALF_DIM, tm), col),
            any_spec, any_spec,
        ],
        out_specs=[
            pl.BlockSpec((tm, ATTN_WIDTH), row),
            pl.BlockSpec((ATTN_WIDTH, tm), col),
            pl.BlockSpec((tm, N_HEADS * VEXT_WIDTH), row),
            pl.BlockSpec((None, 2 * N_HEADS, HEAD_DIM, tm), lambda i: (l, 0, 0, i)),
            pl.BlockSpec((None, tm, N_HEADS, VALUE_DIM), lambda i: (l, i, 0, 0)),
        ],
        out_shape=[
            jax.ShapeDtypeStruct((n, ATTN_WIDTH), BF16),
            jax.ShapeDtypeStruct((ATTN_WIDTH, n), BF16),
            jax.ShapeDtypeStruct((n, N_HEADS * VEXT_WIDTH), BF16),
            jax.ShapeDtypeStruct(kbuf.shape, F32),
            jax.ShapeDtypeStruct(vbuf.shape, F32),
        ],
        input_output_aliases={8: 3, 9: 4},
        compiler_params=_params(("parallel",)),
        name="qkv_rope_prompt",
    )(x, w_in_b, wkt_b, w_in_b, cos_t, sin_t, cost_t, sint_t, kbuf, vbuf)


def _qkv_sample_kernel(x_ref, w_ref, cos_ref, sin_ref, qb_ref, kb_ref, vb_ref, k_ref, v_ref):
    xb = x_ref[...].astype(BF16)
    h = _dot(xb, w_ref[...])
    cos = cos_ref[...]
    sin = sin_ref[...]
    for c in range(ATTN_WIDTH // LANES):
        sl = slice(c * LANES, (c + 1) * LANES)
        q = _rope_lanes(h[:, c * LANES:(c + 1) * LANES], cos, sin)
        qb_ref[:, sl] = (q * SCALE).astype(BF16)
        k = _rope_lanes(h[:, ATTN_WIDTH + c * LANES:ATTN_WIDTH + (c + 1) * LANES], cos, sin)
        k_ref[:, sl] = k
        kb_ref[:, sl] = k.astype(BF16)
    v = h[:, 2 * ATTN_WIDTH:3 * ATTN_WIDTH]
    v_ref[...] = v
    vb_ref[...] = v.astype(BF16)


def _qkv_sample_call(l, x, w_in_b, cos_t, sin_t):
    n = x.shape[0]
    tm = ROW_TILE
    row = lambda i: (i, 0)
    return pl.pallas_call(
        _qkv_sample_kernel,
        grid=(n // tm,),
        in_specs=[
            pl.BlockSpec((tm, D_MODEL), row),
            pl.BlockSpec((None, D_MODEL, QKV_WIDTH), lambda i: (l, 0, 0)),
            pl.BlockSpec((tm, LANES), row),
            pl.BlockSpec((tm, LANES), row),
        ],
        out_specs=[pl.BlockSpec((tm, ATTN_WIDTH), row)] * 5,
        out_shape=[jax.ShapeDtypeStruct((n, ATTN_WIDTH), BF16)] * 3
        + [jax.ShapeDtypeStruct((n, ATTN_WIDTH), F32)] * 2,
        compiler_params=_params(("parallel",)),
        name="qkv_rope_sample",
    )(x, w_in_b, cos_t, sin_t)


def _lambda_value(lq1_ref, lk1_ref, lq2_ref, lk2_ref, lam_init):
    a = jnp.sum(lq1_ref[...] * lk1_ref[...], axis=-1, keepdims=True)
    b = jnp.sum(lq2_ref[...] * lk2_ref[...], axis=-1, keepdims=True)
    return jnp.exp(a) - jnp.exp(b) + lam_init


def _stack_sub_queries(q_ref, qs_ref, rows):
    lane = lax.broadcasted_iota(jnp.int32, (rows, LANES), 1)
    low = lane < HEAD_DIM
    zero = jnp.zeros((rows, LANES), BF16)
    for h in range(N_HEADS):
        qh = q_ref[:, h * LANES:(h + 1) * LANES]
        qs_ref[h, :rows, :] = jnp.where(low, qh, zero)
        qs_ref[h, rows:, :] = jnp.where(low, zero, qh)


def _combine_heads(acc0, acc1, l0, l1, lam, subw, lam_init):
    o = acc0 / l0 - lam * (acc1 / l1)
    ms = jnp.mean(o * o, axis=-1, keepdims=True)
    return o * lax.rsqrt(ms + LN_EPS) * subw * (1.0 - lam_init)


def _attn_prompt_kernel(qi_ref, ki_ref, q_ref, kt_ref, v_ref, lq1_ref, lk1_ref, lq2_ref,
                        lk2_ref, subw_ref, o_ref, qs_ref, m_ref, acc_ref, *,
                        tile, chain, lam_init):
    step = pl.program_id(0)
    qi = qi_ref[step]
    ki = ki_ref[step]

    @pl.when(ki == 0)
    def _():
        _stack_sub_queries(q_ref, qs_ref, tile)
        m_ref[...] = jnp.full(m_ref.shape, -jnp.inf, F32)
        acc_ref[...] = jnp.zeros(acc_ref.shape, F32)

    def update(r0, k0, k1, masked):
        width = k1 - k0
        if masked:
            r = lax.broadcasted_iota(jnp.int32, (chain, width), 0)
            c = lax.broadcasted_iota(jnp.int32, (chain, width), 1)
            visible = (c // CHUNK) <= (r // CHUNK)
        for h in range(N_HEADS):
            for sub in range(2):
                rows = slice(sub * tile + r0, sub * tile + r0 + chain)
                s = _dot(qs_ref[h, rows], kt_ref[h * LANES:(h + 1) * LANES, k0:k1])
                if masked:
                    s = jnp.where(visible, s, -jnp.inf)
                m_prev = m_ref[h, rows]
                m_next = jnp.maximum(m_prev, jnp.max(s, axis=-1, keepdims=True))
                alpha = jnp.exp(m_prev - m_next)
                p = jnp.exp((s - pltpu.repeat(m_next, width // LANES, axis=1)).astype(BF16))
                m_ref[h, rows] = m_next
                pv = _dot(p, v_ref[k0:k1, h * VEXT_WIDTH:(h + 1) * VEXT_WIDTH])
                acc_ref[h, rows] = (pltpu.repeat(alpha, VEXT_WIDTH // LANES, axis=1)
                                    * acc_ref[h, rows] + pv)

    @pl.when(ki < qi)
    def _():
        for rb in range(tile // chain):
            update(rb * chain, 0, tile, False)

    @pl.when(ki == qi)
    def _():
        for rb in range(tile // chain):
            if rb:
                update(rb * chain, 0, rb * chain, False)
            update(rb * chain, rb * chain, (rb + 1) * chain, True)
        lam = _lambda_value(lq1_ref, lk1_ref, lq2_ref, lk2_ref, lam_init)
        subw = subw_ref[...]
        for h in range(N_HEADS):
            acc = acc_ref[h]
            o_ref[:, h * LANES:(h + 1) * LANES] = _combine_heads(
                acc[:tile, :VALUE_DIM], acc[tile:, :VALUE_DIM],
                acc[:tile, VALUE_DIM:], acc[tile:, VALUE_DIM:], lam, subw, lam_init)


def _attn_prompt_call(l, qb, ktb, vb, lam_vecs, subw, lam_init):
    s = qb.shape[0]
    t = ATTN_TILE
    nq = s // t
    qi = np.concatenate([np.full(i + 1, i, np.int32) for i in range(nq)])
    ki = np.concatenate([np.arange(i + 1, dtype=np.int32) for i in range(nq)])
    vec = pl.BlockSpec((None, 1, HEAD_DIM), lambda g, qi, ki: (l, 0, 0))
    grid_spec = pltpu.PrefetchScalarGridSpec(
        num_scalar_prefetch=2,
        grid=(len(qi),),
        in_specs=[
            pl.BlockSpec((t, ATTN_WIDTH), lambda g, qi, ki: (qi[g], 0)),
            pl.BlockSpec((ATTN_WIDTH, t), lambda g, qi, ki: (0, ki[g])),
            pl.BlockSpec((t, N_HEADS * VEXT_WIDTH), lambda g, qi, ki: (ki[g], 0)),
            vec, vec, vec, vec,
            pl.BlockSpec((None, 1, VALUE_DIM), lambda g, qi, ki: (l, 0, 0)),
        ],
        out_specs=pl.BlockSpec((t, ATTN_WIDTH), lambda g, qi, ki: (qi[g], 0)),
        scratch_shapes=[
            pltpu.VMEM((N_HEADS, 2 * t, LANES), BF16),
            pltpu.VMEM((N_HEADS, 2 * t, LANES), F32),
            pltpu.VMEM((N_HEADS, 2 * t, VEXT_WIDTH), F32),
        ],
    )
    return pl.pallas_call(
        functools.partial(_attn_prompt_kernel, tile=t, chain=ATTN_CHAIN, lam_init=lam_init),
        grid_spec=grid_spec,
        out_shape=jax.ShapeDtypeStruct((s, ATTN_WIDTH), F32),
        compiler_params=_params(("arbitrary",)),
        name="diff_attn_prompt",
    )(jnp.asarray(qi), jnp.asarray(ki), qb, ktb, vb, *lam_vecs, subw)


def _attn_sample_kernel(q_ref, k_ref, v_ref, ckt_ref, cv_ref, lq1_ref, lk1_ref, lq2_ref,
                        lk2_ref, subw_ref, o_ref, qs_ref, *, rows, lam_init):
    _stack_sub_queries(q_ref, qs_ref, rows)
    lam = _lambda_value(lq1_ref, lk1_ref, lq2_ref, lk2_ref, lam_init)
    subw = subw_ref[...]
    past = ckt_ref.shape[-1]
    for h in range(N_HEADS):
        hs = slice(h * LANES, (h + 1) * LANES)
        qs = qs_ref[h]
        ckt = ckt_ref[2 * h:2 * h + 2].reshape(LANES, past).astype(BF16)
        s_c = _dot(qs, ckt)
        s_n = _nt_dot(qs, k_ref[:, hs])
        m = jnp.maximum(jnp.max(s_c, axis=-1, keepdims=True),
                        jnp.max(s_n, axis=-1, keepdims=True))
        p_c = jnp.exp(s_c - m)
        p_n = jnp.exp(s_n - m)
        l = jnp.sum(p_c, axis=-1, keepdims=True) + jnp.sum(p_n, axis=-1, keepdims=True)
        acc = _dot(p_c.astype(BF16), cv_ref[:, h, :].astype(BF16))
        acc = acc + _dot(p_n.astype(BF16), v_ref[:, hs])
        o_ref[:, hs] = _combine_heads(acc[:rows], acc[rows:], l[:rows], l[rows:], lam, subw,
                                      lam_init)


def _attn_sample_call(l, qb, kb, vb, ckt, cv, lam_vecs, subw, lam_init, rows):
    n = qb.shape[0] // rows
    past = cv.shape[2]
    vec = pl.BlockSpec((None, 1, HEAD_DIM), lambda i: (l, 0, 0))
    new = pl.BlockSpec((rows, ATTN_WIDTH), lambda i: (i, 0))
    return pl.pallas_call(
        functools.partial(_attn_sample_kernel, rows=rows, lam_init=lam_init),
        grid=(n,),
        in_specs=[
            new, new, new,
            pl.BlockSpec((None, None, 2 * N_HEADS, HEAD_DIM, past), lambda i: (l, i, 0, 0, 0)),
            pl.BlockSpec((None, None, past, N_HEADS, VALUE_DIM), lambda i: (l, i, 0, 0, 0)),
            vec, vec, vec, vec,
            pl.BlockSpec((None, 1, VALUE_DIM), lambda i: (l, 0, 0)),
        ],
        out_specs=new,
        out_shape=jax.ShapeDtypeStruct((n * rows, ATTN_WIDTH), F32),
        scratch_shapes=[pltpu.VMEM((N_HEADS, 2 * rows, LANES), BF16)],
        compiler_params=_params(("parallel",)),
        name="diff_attn_sample",
    )(qb, kb, vb, ckt, cv, *lam_vecs, subw)


_GA = (QKV_WIDTH, QKV_WIDTH + ATTN_WIDTH)
_U = (_GA[1], _GA[1] + GMLP_WIDTH)
_VG = (_U[1], _U[1] + GMLP_WIDTH)
_GG = (_VG[1], _VG[1] + GMLP_WIDTH)
_MA = (_GG[1], _GG[1] + D_MODEL)
_MB = (_MA[1], IN_WIDTH)


def _post_kernel(x_ref, o_ref, w_ref, woa_ref, wog_ref, wout_ref, sg_ref, sb_ref, ws_ref,
                 bs_ref, lng_ref, lnb_ref, *out_and_scratch, period, emit_vn):
    if emit_vn:
        xo_ref, vn_ref, sgu_ref = out_and_scratch
    else:
        xo_ref, sgu_ref = out_and_scratch
    tm = x_ref.shape[0]
    x = x_ref[...]
    xb = x.astype(BF16)

    def proj(cols):
        return _dot(xb, w_ref[:, cols[0]:cols[1]])

    ta = (o_ref[...] * _silu(proj(_GA))).astype(BF16)
    ya = _dot(ta, woa_ref[...])

    vn = _layernorm(proj(_VG), sg_ref[...], sb_ref[...])
    if emit_vn:
        vn_ref[...] = vn
    vnb = vn.astype(BF16)
    r = lax.broadcasted_iota(jnp.int32, (GMLP_CHUNK, GMLP_CHUNK), 0)
    c = lax.broadcasted_iota(jnp.int32, (GMLP_CHUNK, GMLP_CHUNK), 1)
    causal = (c <= r) & ((r // period) == (c // period))
    bias = bs_ref[...]
    for g in range(GMLP_GROUPS):
        gs = slice(g * LANES, (g + 1) * LANES)
        wsg = jnp.where(causal, ws_ref[g], 0.0).astype(BF16)
        for ch in range(tm // GMLP_CHUNK):
            rs = slice(ch * GMLP_CHUNK, (ch + 1) * GMLP_CHUNK)
            sgu_ref[rs, gs] = _dot(wsg, vnb[rs, gs]) + bias[:, gs]
    tg = (proj(_U) * sgu_ref[...] * _silu(proj(_GG))).astype(BF16)
    yg = _dot(tg, wog_ref[...])

    merged = jax.nn.sigmoid(proj(_MA)) * ya + jax.nn.sigmoid(proj(_MB)) * yg
    y = _dot(merged.astype(BF16), wout_ref[...])
    xo_ref[...] = _layernorm(ALPHA * x + y, lng_ref[...], lnb_ref[...])


def _post_call(l, x, o, w_in_b, woa_b, wog_b, wout_b, sg, sb, ws, bs, lng, lnb, period,
               emit_vn):
    n = x.shape[0]
    tm = ROW_TILE
    row = lambda i: (i, 0)
    layer3 = lambda i: (l, 0, 0)
    out_specs = [pl.BlockSpec((tm, D_MODEL), row)]
    out_shape = [jax.ShapeDtypeStruct((n, D_MODEL), F32)]
    if emit_vn:
        out_specs.append(pl.BlockSpec((tm, GMLP_WIDTH), row))
        out_shape.append(jax.ShapeDtypeStruct((n, GMLP_WIDTH), F32))
    return pl.pallas_call(
        functools.partial(_post_kernel, period=period, emit_vn=emit_vn),
        grid=(n // tm,),
        in_specs=[
            pl.BlockSpec((tm, D_MODEL), row),
            pl.BlockSpec((tm, ATTN_WIDTH), row),
            pl.BlockSpec((None, D_MODEL, IN_WIDTH), layer3),
            pl.BlockSpec((None, ATTN_WIDTH, D_MODEL), layer3),
            pl.BlockSpec((None, GMLP_WIDTH, D_MODEL), layer3),
            pl.BlockSpec((None, D_MODEL, D_MODEL), layer3),
            pl.BlockSpec((None, 1, GMLP_WIDTH), layer3),
            pl.BlockSpec((None, 1, GMLP_WIDTH), layer3),
            pl.BlockSpec((None, GMLP_GROUPS, GMLP_CHUNK, GMLP_CHUNK), lambda i: (l, 0, 0, 0)),
            pl.BlockSpec((None, GMLP_CHUNK, GMLP_WIDTH), layer3),
            pl.BlockSpec((None, 1, D_MODEL), layer3),
            pl.BlockSpec((None, 1, D_MODEL), layer3),
        ],
        out_specs=out_specs,
        out_shape=out_shape,
        scratch_shapes=[pltpu.VMEM((tm, GMLP_WIDTH), F32)],
        compiler_params=_params(("parallel",)),
        name="post_mix",
    )(x, o, w_in_b, woa_b, wog_b, wout_b, sg, sb, ws, bs, lng, lnb)


def _rope_angles(pos):
    inv = ROPE_THETA ** (-jnp.arange(HALF_DIM, dtype=F32) / HALF_DIM)
    ang = pos.astype(F32)[:, None] * inv[None, :]
    return jnp.cos(ang), jnp.sin(ang)


def _lane_tables(cos, sin):
    reps = LANES // HEAD_DIM
    cos_t = jnp.tile(jnp.concatenate([cos, cos], -1), (1, reps))
    sin_t = jnp.tile(jnp.concatenate([-sin, sin], -1), (1, reps))
    return cos_t, sin_t


def kernel(x_prompt, x_sample, cache_k, cache_v, w_in, w_oa, w_og, w_out, lambda_q1,
           lambda_k1, lambda_q2, lambda_k2, subln_w, sgu_ln_g, sgu_ln_b, w_s, b_s, ln_g, ln_b):
    batch, seq, _ = x_prompt.shape
    dec_batch, dec_seq, _ = x_sample.shape
    assert batch == 1 and seq % ATTN_TILE == 0 and seq % ROW_TILE == 0
    assert (dec_batch * dec_seq) % ROW_TILE == 0 and GMLP_CHUNK % dec_seq == 0

    cos_p, sin_p = _rope_angles(jnp.arange(seq))
    cos_pl, sin_pl = _lane_tables(cos_p, sin_p)
    cos_pt, sin_pt = cos_p.T, sin_p.T
    cos_s, sin_s = _lane_tables(*_rope_angles(PAST_LEN + jnp.arange(dec_seq)))
    cos_s = jnp.tile(cos_s, (dec_batch, 1))
    sin_s = jnp.tile(sin_s, (dec_batch, 1))

    xp = x_prompt.reshape(seq, D_MODEL)
    xs = x_sample.reshape(dec_batch * dec_seq, D_MODEL)
    ckt = jnp.transpose(cache_k, (0, 1, 3, 4, 2))

    w_in_b = w_in.astype(BF16)
    wkt_b = jnp.transpose(w_in[:, :, ATTN_WIDTH:2 * ATTN_WIDTH], (0, 2, 1)).astype(BF16)
    woa_b = w_oa.astype(BF16)
    wog_b = w_og.astype(BF16)
    wout_b = w_out.astype(BF16)
    lam_vecs = tuple(v.reshape(DEPTH, 1, HEAD_DIM)
                     for v in (lambda_q1, lambda_k1, lambda_q2, lambda_k2))
    subw = subln_w.reshape(DEPTH, 1, VALUE_DIM)
    sg = sgu_ln_g.reshape(DEPTH, 1, GMLP_WIDTH)
    sb = sgu_ln_b.reshape(DEPTH, 1, GMLP_WIDTH)
    lng = ln_g.reshape(DEPTH, 1, D_MODEL)
    lnb = ln_b.reshape(DEPTH, 1, D_MODEL)
    bias_p = jnp.repeat(jnp.transpose(b_s, (0, 2, 1)), GMLP_WIDTH // GMLP_GROUPS, axis=2)
    reps = GMLP_CHUNK // dec_seq
    ws_s = jnp.tile(w_s[:, :, :dec_seq, :dec_seq], (1, 1, reps, reps))
    bias_s = jnp.tile(bias_p[:, :dec_seq], (1, reps, 1))

    kbuf = jnp.zeros((DEPTH, 2 * N_HEADS, HEAD_DIM, seq), F32)
    vbuf = jnp.zeros((DEPTH, seq, N_HEADS, VALUE_DIM), F32)

    ks_l, vs_l, gs_l = [], [], []
    for l in range(DEPTH):
        lam_init = 0.8 - 0.6 * math.exp(-0.3 * l)

        qb, ktb, vb, kbuf, vbuf = _qkv_prompt_call(l, xp, w_in_b, wkt_b, cos_pl, sin_pl,
                                                   cos_pt, sin_pt, kbuf, vbuf)
        op = _attn_prompt_call(l, qb, ktb, vb, lam_vecs, subw, lam_init)
        (xp,) = _post_call(l, xp, op, w_in_b, woa_b, wog_b, wout_b, sg, sb, w_s, bias_p,
                           lng, lnb, GMLP_CHUNK, False)

        qb, kb, vb, ksn, vsn = _qkv_sample_call(l, xs, w_in_b, cos_s, sin_s)
        osn = _attn_sample_call(l, qb, kb, vb, ckt, cache_v, lam_vecs, subw, lam_init, dec_seq)
        xs, gsn = _post_call(l, xs, osn, w_in_b, woa_b, wog_b, wout_b, sg, sb, ws_s, bias_s,
                             lng, lnb, dec_seq, True)
        ks_l.append(ksn); vs_l.append(vsn); gs_l.append(gsn)

    new_k_prompt = jnp.transpose(kbuf.reshape(DEPTH, batch, 2 * N_HEADS, HEAD_DIM, seq),
                                 (0, 1, 4, 2, 3))
    new_v_prompt = vbuf.reshape(DEPTH, batch, seq, N_HEADS, VALUE_DIM)
    new_k_sample = jnp.stack(ks_l).reshape(DEPTH, dec_batch, dec_seq, 2 * N_HEADS, HEAD_DIM)
    new_v_sample = jnp.stack(vs_l).reshape(DEPTH, dec_batch, dec_seq, N_HEADS, VALUE_DIM)
    new_gv_sample = jnp.stack(gs_l).reshape(DEPTH, dec_batch, dec_seq, GMLP_WIDTH)
    return (xp.reshape(batch, seq, D_MODEL), xs.reshape(dec_batch, dec_seq, D_MODEL),
            new_k_prompt, new_v_prompt, new_k_sample, new_v_sample, new_gv_sample)
```

```python
import functools
import math

import jax
import jax.numpy as jnp
import numpy as np
from jax import lax
from jax.experimental import pallas as pl
from jax.experimental.pallas import tpu as pltpu

D_MODEL = 1024
DEPTH = 4
PAST_LEN = 1024
CHUNK = 64
N_HEADS = 4
HEAD_DIM = 64
HALF_DIM = HEAD_DIM // 2
VALUE_DIM = 2 * HEAD_DIM
ATTN_WIDTH = N_HEADS * VALUE_DIM
GMLP_CHUNK = 128
GMLP_GROUPS = 4
GMLP_WIDTH = 512
ROPE_THETA = 10000.0
ALPHA = (2 * DEPTH) ** 0.25
LN_EPS = 1e-5
SCALE = HEAD_DIM ** -0.5
QKV_WIDTH = 3 * ATTN_WIDTH
IN_WIDTH = QKV_WIDTH + ATTN_WIDTH + 3 * GMLP_WIDTH + 2 * D_MODEL

LANES = 128
VMEM_LIMIT_BYTES = 56 * 1024 * 1024

ROW_TILE = 512
ATTN_TILE = 1024
ATTN_K_TILE = 2048
ATTN_CHAIN = 512
VEXT_WIDTH = 2 * VALUE_DIM

F32 = jnp.float32
BF16 = jnp.bfloat16


def _nt_dot(a, b):
    return lax.dot_general(a, b, (((1,), (1,)), ((), ())), preferred_element_type=F32)


def _dot(a, b):
    return jnp.dot(a, b, preferred_element_type=F32)


def _layernorm(x, g, b):
    mu = jnp.mean(x, axis=-1, keepdims=True)
    xc = x - mu
    var = jnp.mean(xc * xc, axis=-1, keepdims=True)
    return xc * lax.rsqrt(var + LN_EPS) * g + b


def _silu(x):
    return x * jax.nn.sigmoid(x)


def _params(semantics):
    return pltpu.CompilerParams(dimension_semantics=semantics,
                                vmem_limit_bytes=VMEM_LIMIT_BYTES)


def _rope_lanes(z, cos, sin):
    lane = lax.broadcasted_iota(jnp.int32, z.shape, 1)
    first_half = (lane % HEAD_DIM) < HALF_DIM
    partner = jnp.where(first_half, pltpu.roll(z, LANES - HALF_DIM, 1),
                        pltpu.roll(z, HALF_DIM, 1))
    return z * cos + partner * sin


def _qkv_prompt_kernel(x_ref, wq_ref, wkt_ref, wv_ref, cos_ref, sin_ref, cost_ref, sint_ref,
                       kbuf_ref, vbuf_ref, qb_ref, ktb_ref, vb_ref, kt_ref, v_ref):
    del kbuf_ref, vbuf_ref
    xb = x_ref[...].astype(BF16)
    cos = cos_ref[...]
    sin = sin_ref[...]
    q = _dot(xb, wq_ref[...])
    for c in range(ATTN_WIDTH // LANES):
        sl = slice(c * LANES, (c + 1) * LANES)
        qb_ref[:, sl] = (_rope_lanes(q[:, sl], cos, sin) * SCALE).astype(BF16)

    kt = _nt_dot(wkt_ref[...], xb)
    cost = cost_ref[...]
    sint = sint_ref[...]
    for j in range(2 * N_HEADS):
        x1 = kt[j * HEAD_DIM:j * HEAD_DIM + HALF_DIM]
        x2 = kt[j * HEAD_DIM + HALF_DIM:(j + 1) * HEAD_DIM]
        o1 = x1 * cost - x2 * sint
        o2 = x2 * cost + x1 * sint
        kt_ref[j, :HALF_DIM, :] = o1
        kt_ref[j, HALF_DIM:, :] = o2
        ktb_ref[j * HEAD_DIM:j * HEAD_DIM + HALF_DIM, :] = o1.astype(BF16)
        ktb_ref[j * HEAD_DIM + HALF_DIM:(j + 1) * HEAD_DIM, :] = o2.astype(BF16)

    v = _dot(xb, wv_ref[...])
    rows = v.shape[0]
    ones = jnp.ones((rows, VALUE_DIM), BF16)
    for h in range(N_HEADS):
        vh = v[:, h * LANES:(h + 1) * LANES]
        v_ref[pl.ds(h, rows, stride=N_HEADS), :] = vh
        vb_ref[:, h * VEXT_WIDTH:h * VEXT_WIDTH + VALUE_DIM] = vh.astype(BF16)
        vb_ref[:, h * VEXT_WIDTH + VALUE_DIM:(h + 1) * VEXT_WIDTH] = ones


def _qkv_prompt_call(l, x, w_in_b, wkt_b, cos_t, sin_t, cost_t, sint_t, kbuf, vbuf):
    n = x.shape[0]
    tm = ROW_TILE
    row = lambda i: (i, 0)
    col = lambda i: (0, i)
    any_spec = pl.BlockSpec(memory_space=pl.ANY)
    return pl.pallas_call(
        _qkv_prompt_kernel,
        grid=(n // tm,),
        in_specs=[
            pl.BlockSpec((tm, D_MODEL), row),
            pl.BlockSpec((None, D_MODEL, ATTN_WIDTH), lambda i: (l, 0, 0)),
            pl.BlockSpec((None, ATTN_WIDTH, D_MODEL), lambda i: (l, 0, 0)),
            pl.BlockSpec((None, D_MODEL, ATTN_WIDTH), lambda i: (l, 0, 2)),
            pl.BlockSpec((tm, LANES), row),
            pl.BlockSpec((tm, LANES), row),
            pl.BlockSpec((HALF_DIM, tm), col),
            pl.BlockSpec((HALF_DIM, tm), col),
            any_spec, any_spec,
        ],
        out_specs=[
            pl.BlockSpec((tm, ATTN_WIDTH), row),
            pl.BlockSpec((ATTN_WIDTH, tm), col),
            pl.BlockSpec((tm, N_HEADS * VEXT_WIDTH), row),
            pl.BlockSpec((None, 2 * N_HEADS, HEAD_DIM, tm), lambda i: (l, 0, 0, i)),
            pl.BlockSpec((None, tm * N_HEADS, VALUE_DIM), lambda i: (l, i, 0)),
        ],
        out_shape=[
            jax.ShapeDtypeStruct((n, ATTN_WIDTH), BF16),
            jax.ShapeDtypeStruct((ATTN_WIDTH, n), BF16),
            jax.ShapeDtypeStruct((n, N_HEADS * VEXT_WIDTH), BF16),
            jax.ShapeDtypeStruct(kbuf.shape, F32),
            jax.ShapeDtypeStruct(vbuf.shape, F32),
        ],
        input_output_aliases={8: 3, 9: 4},
        compiler_params=_params(("parallel",)),
        name="qkv_rope_prompt",
    )(x, w_in_b, wkt_b, w_in_b, cos_t, sin_t, cost_t, sint_t, kbuf, vbuf)


def _qkv_sample_kernel(x_ref, w_ref, cos_ref, sin_ref, qb_ref, kb_ref, vb_ref, k_ref, v_ref):
    xb = x_ref[...].astype(BF16)
    h = _dot(xb, w_ref[...])
    cos = cos_ref[...]
    sin = sin_ref[...]
    for c in range(ATTN_WIDTH // LANES):
        sl = slice(c * LANES, (c + 1) * LANES)
        q = _rope_lanes(h[:, c * LANES:(c + 1) * LANES], cos, sin)
        qb_ref[:, sl] = (q * SCALE).astype(BF16)
        k = _rope_lanes(h[:, ATTN_WIDTH + c * LANES:ATTN_WIDTH + (c + 1) * LANES], cos, sin)
        k_ref[:, sl] = k
        kb_ref[:, sl] = k.astype(BF16)
    v = h[:, 2 * ATTN_WIDTH:3 * ATTN_WIDTH]
    v_ref[...] = v
    vb_ref[...] = v.astype(BF16)


def _qkv_sample_call(l, x, w_in_b, cos_t, sin_t):
    n = x.shape[0]
    tm = ROW_TILE
    row = lambda i: (i, 0)
    return pl.pallas_call(
        _qkv_sample_kernel,
        grid=(n // tm,),
        in_specs=[
            pl.BlockSpec((tm, D_MODEL), row),
            pl.BlockSpec((None, D_MODEL, QKV_WIDTH), lambda i: (l, 0, 0)),
            pl.BlockSpec((tm, LANES), row),
            pl.BlockSpec((tm, LANES), row),
        ],
        out_specs=[pl.BlockSpec((tm, ATTN_WIDTH), row)] * 5,
        out_shape=[jax.ShapeDtypeStruct((n, ATTN_WIDTH), BF16)] * 3
        + [jax.ShapeDtypeStruct((n, ATTN_WIDTH), F32)] * 2,
        compiler_params=_params(("parallel",)),
        name="qkv_rope_sample",
    )(x, w_in_b, cos_t, sin_t)


def _lambda_value(lq1_ref, lk1_ref, lq2_ref, lk2_ref, lam_init):
    a = jnp.sum(lq1_ref[...] * lk1_ref[...], axis=-1, keepdims=True)
    b = jnp.sum(lq2_ref[...] * lk2_ref[...], axis=-1, keepdims=True)
    return jnp.exp(a) - jnp.exp(b) + lam_init


def _stack_sub_queries(q_ref, qs_ref, rows):
    lane = lax.broadcasted_iota(jnp.int32, (rows, LANES), 1)
    low = lane < HEAD_DIM
    zero = jnp.zeros((rows, LANES), BF16)
    for h in range(N_HEADS):
        qh = q_ref[:, h * LANES:(h + 1) * LANES]
        qs_ref[h, :rows, :] = jnp.where(low, qh, zero)
        qs_ref[h, rows:, :] = jnp.where(low, zero, qh)


def _combine_heads(acc0, acc1, l0, l1, lam, subw, lam_init):
    o = acc0 / l0 - lam * (acc1 / l1)
    ms = jnp.mean(o * o, axis=-1, keepdims=True)
    return o * lax.rsqrt(ms + LN_EPS) * subw * (1.0 - lam_init)


def _attn_prompt_kernel(qi_ref, ki_ref, q_ref, kt_ref, v_ref, lq1_ref, lk1_ref, lq2_ref,
                        lk2_ref, subw_ref, o_ref, qs_ref, m_ref, acc_ref, *,
                        tile, ktile, chain, lam_init):
    ratio = ktile // tile
    step = pl.program_id(0)
    qi = qi_ref[step]
    ki = ki_ref[step]
    last = qi // ratio

    @pl.when(ki == 0)
    def _():
        _stack_sub_queries(q_ref, qs_ref, tile)
        m_ref[...] = jnp.full(m_ref.shape, -jnp.inf, F32)
        acc_ref[...] = jnp.zeros(acc_ref.shape, F32)

    def update(r0, k0, k1, masked):
        width = k1 - k0
        if masked:
            r = lax.broadcasted_iota(jnp.int32, (chain, width), 0)
            c = lax.broadcasted_iota(jnp.int32, (chain, width), 1)
            visible = (c // CHUNK) <= (r // CHUNK)
        for h in range(N_HEADS):
            for sub in range(2):
                rows = slice(sub * tile + r0, sub * tile + r0 + chain)
                s = _dot(qs_ref[h, rows], kt_ref[h * LANES:(h + 1) * LANES, k0:k1])
                if masked:
                    s = jnp.where(visible, s, -jnp.inf)
                m_prev = m_ref[h, rows]
                m_next = jnp.maximum(m_prev, jnp.max(s, axis=-1, keepdims=True))
                alpha = jnp.exp(m_prev - m_next)
                p = jnp.exp((s - pltpu.repeat(m_next, width // LANES, axis=1)).astype(BF16))
                m_ref[h, rows] = m_next
                pv = _dot(p, v_ref[k0:k1, h * VEXT_WIDTH:(h + 1) * VEXT_WIDTH])
                acc_ref[h, rows] = (pltpu.repeat(alpha, VEXT_WIDTH // LANES, axis=1)
                                    * acc_ref[h, rows] + pv)

    @pl.when(ki < last)
    def _():
        for rb in range(tile // chain):
            update(rb * chain, 0, ktile, False)

    for part in range(ratio):
        @pl.when((ki == last) & (qi % ratio == part))
        def _():
            for rb in range(tile // chain):
                diag = part * tile + rb * chain
                if diag:
                    update(rb * chain, 0, diag, False)
                update(rb * chain, diag, diag + chain, True)
            lam = _lambda_value(lq1_ref, lk1_ref, lq2_ref, lk2_ref, lam_init)
            subw = subw_ref[...]
            for h in range(N_HEADS):
                acc = acc_ref[h]
                o_ref[:, h * LANES:(h + 1) * LANES] = _combine_heads(
                    acc[:tile, :VALUE_DIM], acc[tile:, :VALUE_DIM],
                    acc[:tile, VALUE_DIM:], acc[tile:, VALUE_DIM:], lam, subw, lam_init)


def _attn_prompt_call(l, qb, ktb, vb, lam_vecs, subw, lam_init):
    s = qb.shape[0]
    t, kt = ATTN_TILE, ATTN_K_TILE
    ratio = kt // t
    nq = s // t
    qi = np.concatenate([np.full(i // ratio + 1, i, np.int32) for i in range(nq)])
    ki = np.concatenate([np.arange(i // ratio + 1, dtype=np.int32) for i in range(nq)])
    vec = pl.BlockSpec((None, 1, HEAD_DIM), lambda g, qi, ki: (l, 0, 0))
    grid_spec = pltpu.PrefetchScalarGridSpec(
        num_scalar_prefetch=2,
        grid=(len(qi),),
        in_specs=[
            pl.BlockSpec((t, ATTN_WIDTH), lambda g, qi, ki: (qi[g], 0)),
            pl.BlockSpec((ATTN_WIDTH, kt), lambda g, qi, ki: (0, ki[g])),
            pl.BlockSpec((kt, N_HEADS * VEXT_WIDTH), lambda g, qi, ki: (ki[g], 0)),
            vec, vec, vec, vec,
            pl.BlockSpec((None, 1, VALUE_DIM), lambda g, qi, ki: (l, 0, 0)),
        ],
        out_specs=pl.BlockSpec((t, ATTN_WIDTH), lambda g, qi, ki: (qi[g], 0)),
        scratch_shapes=[
            pltpu.VMEM((N_HEADS, 2 * t, LANES), BF16),
            pltpu.VMEM((N_HEADS, 2 * t, LANES), F32),
            pltpu.VMEM((N_HEADS, 2 * t, VEXT_WIDTH), F32),
        ],
    )
    return pl.pallas_call(
        functools.partial(_attn_prompt_kernel, tile=t, ktile=kt, chain=ATTN_CHAIN,
                          lam_init=lam_init),
        grid_spec=grid_spec,
        out_shape=jax.ShapeDtypeStruct((s, ATTN_WIDTH), F32),
        compiler_params=_params(("arbitrary",)),
        name="diff_attn_prompt",
    )(jnp.asarray(qi), jnp.asarray(ki), qb, ktb, vb, *lam_vecs, subw)


def _attn_sample_kernel(q_ref, k_ref, v_ref, ckt_ref, cv_ref, lq1_ref, lk1_ref, lq2_ref,
                        lk2_ref, subw_ref, o_ref, qs_ref, *, rows, lam_init):
    _stack_sub_queries(q_ref, qs_ref, rows)
    lam = _lambda_value(lq1_ref, lk1_ref, lq2_ref, lk2_ref, lam_init)
    subw = subw_ref[...]
    past = ckt_ref.shape[-1]
    for h in range(N_HEADS):
        hs = slice(h * LANES, (h + 1) * LANES)
        qs = qs_ref[h]
        ckt = ckt_ref[2 * h:2 * h + 2].reshape(LANES, past).astype(BF16)
        s_c = _dot(qs, ckt)
        s_n = _nt_dot(qs, k_ref[:, hs])
        m = jnp.maximum(jnp.max(s_c, axis=-1, keepdims=True),
                        jnp.max(s_n, axis=-1, keepdims=True))
        p_c = jnp.exp(s_c - m)
        p_n = jnp.exp(s_n - m)
        l = jnp.sum(p_c, axis=-1, keepdims=True) + jnp.sum(p_n, axis=-1, keepdims=True)
        cv_h = cv_ref[pl.ds(h, past, stride=N_HEADS), :]
        acc = _dot(p_c.astype(BF16), cv_h.astype(BF16))
        acc = acc + _dot(p_n.astype(BF16), v_ref[:, hs])
        o_ref[:, hs] = _combine_heads(acc[:rows], acc[rows:], l[:rows], l[rows:], lam, subw,
                                      lam_init)


def _attn_sample_call(l, qb, kb, vb, ckt, cv, lam_vecs, subw, lam_init, rows):
    n = qb.shape[0] // rows
    past = cv.shape[2] // N_HEADS
    vec = pl.BlockSpec((None, 1, HEAD_DIM), lambda i: (l, 0, 0))
    new = pl.BlockSpec((rows, ATTN_WIDTH), lambda i: (i, 0))
    return pl.pallas_call(
        functools.partial(_attn_sample_kernel, rows=rows, lam_init=lam_init),
        grid=(n,),
        in_specs=[
            new, new, new,
            pl.BlockSpec((None, None, 2 * N_HEADS, HEAD_DIM, past), lambda i: (l, i, 0, 0, 0)),
            pl.BlockSpec((None, None, past * N_HEADS, VALUE_DIM), lambda i: (l, i, 0, 0)),
            vec, vec, vec, vec,
            pl.BlockSpec((None, 1, VALUE_DIM), lambda i: (l, 0, 0)),
        ],
        out_specs=new,
        out_shape=jax.ShapeDtypeStruct((n * rows, ATTN_WIDTH), F32),
        scratch_shapes=[pltpu.VMEM((N_HEADS, 2 * rows, LANES), BF16)],
        compiler_params=_params(("parallel",)),
        name="diff_attn_sample",
    )(qb, kb, vb, ckt, cv, *lam_vecs, subw)


_GA = (QKV_WIDTH, QKV_WIDTH + ATTN_WIDTH)
_U = (_GA[1], _GA[1] + GMLP_WIDTH)
_VG = (_U[1], _U[1] + GMLP_WIDTH)
_GG = (_VG[1], _VG[1] + GMLP_WIDTH)
_MA = (_GG[1], _GG[1] + D_MODEL)
_MB = (_MA[1], IN_WIDTH)


def _post_kernel(x_ref, o_ref, w_ref, woa_ref, wog_ref, wout_ref, sg_ref, sb_ref, ws_ref,
                 bs_ref, lng_ref, lnb_ref, *out_and_scratch, period, emit_vn):
    if emit_vn:
        xo_ref, vn_ref, sgu_ref = out_and_scratch
    else:
        xo_ref, sgu_ref = out_and_scratch
    tm = x_ref.shape[0]
    x = x_ref[...]
    xb = x.astype(BF16)

    def proj(cols):
        return _dot(xb, w_ref[:, cols[0]:cols[1]])

    ta = (o_ref[...] * _silu(proj(_GA))).astype(BF16)
    ya = _dot(ta, woa_ref[...])

    vn = _layernorm(proj(_VG), sg_ref[...], sb_ref[...])
    if emit_vn:
        vn_ref[...] = vn
    vnb = vn.astype(BF16)
    r = lax.broadcasted_iota(jnp.int32, (GMLP_CHUNK, GMLP_CHUNK), 0)
    c = lax.broadcasted_iota(jnp.int32, (GMLP_CHUNK, GMLP_CHUNK), 1)
    causal = (c <= r) & ((r // period) == (c // period))
    bias = bs_ref[...]
    for g in range(GMLP_GROUPS):
        gs = slice(g * LANES, (g + 1) * LANES)
        wsg = jnp.where(causal, ws_ref[g], 0.0).astype(BF16)
        for ch in range(tm // GMLP_CHUNK):
            rs = slice(ch * GMLP_CHUNK, (ch + 1) * GMLP_CHUNK)
            sgu_ref[rs, gs] = _dot(wsg, vnb[rs, gs]) + bias[:, gs]
    tg = (proj(_U) * sgu_ref[...] * _silu(proj(_GG))).astype(BF16)
    yg = _dot(tg, wog_ref[...])

    merged = jax.nn.sigmoid(proj(_MA)) * ya + jax.nn.sigmoid(proj(_MB)) * yg
    y = _dot(merged.astype(BF16), wout_ref[...])
    xo_ref[...] = _layernorm(ALPHA * x + y, lng_ref[...], lnb_ref[...])


def _post_call(l, x, o, w_in_b, woa_b, wog_b, wout_b, sg, sb, ws, bs, lng, lnb, period,
               emit_vn):
    n = x.shape[0]
    tm = ROW_TILE
    row = lambda i: (i, 0)
    layer3 = lambda i: (l, 0, 0)
    out_specs = [pl.BlockSpec((tm, D_MODEL), row)]
    out_shape = [jax.ShapeDtypeStruct((n, D_MODEL), F32)]
    if emit_vn:
        out_specs.append(pl.BlockSpec((tm, GMLP_WIDTH), row))
        out_shape.append(jax.ShapeDtypeStruct((n, GMLP_WIDTH), F32))
    return pl.pallas_call(
        functools.partial(_post_kernel, period=period, emit_vn=emit_vn),
        grid=(n // tm,),
        in_specs=[
            pl.BlockSpec((tm, D_MODEL), row),
            pl.BlockSpec((tm, ATTN_WIDTH), row),
            pl.BlockSpec((None, D_MODEL, IN_WIDTH), layer3),
            pl.BlockSpec((None, ATTN_WIDTH, D_MODEL), layer3),
            pl.BlockSpec((None, GMLP_WIDTH, D_MODEL), layer3),
            pl.BlockSpec((None, D_MODEL, D_MODEL), layer3),
            pl.BlockSpec((None, 1, GMLP_WIDTH), layer3),
            pl.BlockSpec((None, 1, GMLP_WIDTH), layer3),
            pl.BlockSpec((None, GMLP_GROUPS, GMLP_CHUNK, GMLP_CHUNK), lambda i: (l, 0, 0, 0)),
            pl.BlockSpec((None, GMLP_CHUNK, GMLP_WIDTH), layer3),
            pl.BlockSpec((None, 1, D_MODEL), layer3),
            pl.BlockSpec((None, 1, D_MODEL), layer3),
        ],
        out_specs=out_specs,
        out_shape=out_shape,
        scratch_shapes=[pltpu.VMEM((tm, GMLP_WIDTH), F32)],
        compiler_params=_params(("parallel",)),
        name="post_mix",
    )(x, o, w_in_b, woa_b, wog_b, wout_b, sg, sb, ws, bs, lng, lnb)


def _rope_angles(pos):
    inv = ROPE_THETA ** (-jnp.arange(HALF_DIM, dtype=F32) / HALF_DIM)
    ang = pos.astype(F32)[:, None] * inv[None, :]
    return jnp.cos(ang), jnp.sin(ang)


def _lane_tables(cos, sin):
    reps = LANES // HEAD_DIM
    cos_t = jnp.tile(jnp.concatenate([cos, cos], -1), (1, reps))
    sin_t = jnp.tile(jnp.concatenate([-sin, sin], -1), (1, reps))
    return cos_t, sin_t


def kernel(x_prompt, x_sample, cache_k, cache_v, w_in, w_oa, w_og, w_out, lambda_q1,
           lambda_k1, lambda_q2, lambda_k2, subln_w, sgu_ln_g, sgu_ln_b, w_s, b_s, ln_g, ln_b):
    batch, seq, _ = x_prompt.shape
    dec_batch, dec_seq, _ = x_sample.shape
    assert batch == 1 and seq % ATTN_K_TILE == 0 and seq % ROW_TILE == 0
    assert (dec_batch * dec_seq) % ROW_TILE == 0 and GMLP_CHUNK % dec_seq == 0

    cos_p, sin_p = _rope_angles(jnp.arange(seq))
    cos_pl, sin_pl = _lane_tables(cos_p, sin_p)
    cos_pt, sin_pt = cos_p.T, sin_p.T
    cos_s, sin_s = _lane_tables(*_rope_angles(PAST_LEN + jnp.arange(dec_seq)))
    cos_s = jnp.tile(cos_s, (dec_batch, 1))
    sin_s = jnp.tile(sin_s, (dec_batch, 1))

    xp = x_prompt.reshape(seq, D_MODEL)
    xs = x_sample.reshape(dec_batch * dec_seq, D_MODEL)
    ckt = jnp.transpose(cache_k, (0, 1, 3, 4, 2))
    cv = cache_v.reshape(DEPTH, dec_batch, PAST_LEN * N_HEADS, VALUE_DIM)

    w_in_b = w_in.astype(BF16)
    wkt_b = jnp.transpose(w_in[:, :, ATTN_WIDTH:2 * ATTN_WIDTH], (0, 2, 1)).astype(BF16)
    woa_b = w_oa.astype(BF16)
    wog_b = w_og.astype(BF16)
    wout_b = w_out.astype(BF16)
    lam_vecs = tuple(v.reshape(DEPTH, 1, HEAD_DIM)
                     for v in (lambda_q1, lambda_k1, lambda_q2, lambda_k2))
    subw = subln_w.reshape(DEPTH, 1, VALUE_DIM)
    sg = sgu_ln_g.reshape(DEPTH, 1, GMLP_WIDTH)
    sb = sgu_ln_b.reshape(DEPTH, 1, GMLP_WIDTH)
    lng = ln_g.reshape(DEPTH, 1, D_MODEL)
    lnb = ln_b.reshape(DEPTH, 1, D_MODEL)
    bias_p = jnp.repeat(jnp.transpose(b_s, (0, 2, 1)), GMLP_WIDTH // GMLP_GROUPS, axis=2)
    reps = GMLP_CHUNK // dec_seq
    ws_s = jnp.tile(w_s[:, :, :dec_seq, :dec_seq], (1, 1, reps, reps))
    bias_s = jnp.tile(bias_p[:, :dec_seq], (1, reps, 1))

    kbuf = jnp.zeros((DEPTH, 2 * N_HEADS, HEAD_DIM, seq), F32)
    vbuf = jnp.zeros((DEPTH, seq * N_HEADS, VALUE_DIM), F32)

    ks_l, vs_l, gs_l = [], [], []
    for l in range(DEPTH):
        lam_init = 0.8 - 0.6 * math.exp(-0.3 * l)

        qb, ktb, vb, kbuf, vbuf = _qkv_prompt_call(l, xp, w_in_b, wkt_b, cos_pl, sin_pl,
                                                   cos_pt, sin_pt, kbuf, vbuf)
        op = _attn_prompt_call(l, qb, ktb, vb, lam_vecs, subw, lam_init)
        (xp,) = _post_call(l, xp, op, w_in_b, woa_b, wog_b, wout_b, sg, sb, w_s, bias_p,
                           lng, lnb, GMLP_CHUNK, False)

        qb, kb, vb, ksn, vsn = _qkv_sample_call(l, xs, w_in_b, cos_s, sin_s)
        osn = _attn_sample_call(l, qb, kb, vb, ckt, cv, lam_vecs, subw, lam_init, dec_seq)
        xs, gsn = _post_call(l, xs, osn, w_in_b, woa_b, wog_b, wout_b, sg, sb, ws_s, bias_s,
                             lng, lnb, dec_seq, True)
        ks_l.append(ksn); vs_l.append(vsn); gs_l.append(gsn)

    new_k_prompt = jnp.transpose(kbuf.reshape(DEPTH, batch, 2 * N_HEADS, HEAD_DIM, seq),
                                 (0, 1, 4, 2, 3))
    new_v_prompt = vbuf.reshape(DEPTH, batch, seq, N_HEADS, VALUE_DIM)
    new_k_sample = jnp.stack(ks_l).reshape(DEPTH, dec_batch, dec_seq, 2 * N_HEADS, HEAD_DIM)
    new_v_sample = jnp.stack(vs_l).reshape(DEPTH, dec_batch, dec_seq, N_HEADS, VALUE_DIM)
    new_gv_sample = jnp.stack(gs_l).reshape(DEPTH, dec_batch, dec_seq, GMLP_WIDTH)
    return (xp.reshape(batch, seq, D_MODEL), xs.reshape(dec_batch, dec_seq, D_MODEL),
            new_k_prompt, new_v_prompt, new_k_sample, new_v_sample, new_gv_sample)
```

```python
import functools
import math

import jax
import jax.numpy as jnp
import numpy as np
from jax import lax
from jax.experimental import pallas as pl
from jax.experimental.pallas import tpu as pltpu

D_MODEL = 1024
DEPTH = 4
PAST_LEN = 1024
CHUNK = 64
N_HEADS = 4
HEAD_DIM = 64
HALF_DIM = HEAD_DIM // 2
VALUE_DIM = 2 * HEAD_DIM
ATTN_WIDTH = N_HEADS * VALUE_DIM
GMLP_CHUNK = 128
GMLP_GROUPS = 4
GMLP_WIDTH = 512
ROPE_THETA = 10000.0
ALPHA = (2 * DEPTH) ** 0.25
LN_EPS = 1e-5
SCALE = HEAD_DIM ** -0.5
QKV_WIDTH = 3 * ATTN_WIDTH
IN_WIDTH = QKV_WIDTH + ATTN_WIDTH + 3 * GMLP_WIDTH + 2 * D_MODEL

LANES = 128
VMEM_LIMIT_BYTES = 56 * 1024 * 1024

ROW_TILE = 512
POST_TILE = 1024
ATTN_TILE = 1024
ATTN_K_TILE = 1024
ATTN_CHAIN = 512
VEXT_WIDTH = 2 * VALUE_DIM

F32 = jnp.float32
BF16 = jnp.bfloat16


def _nt_dot(a, b):
    return lax.dot_general(a, b, (((1,), (1,)), ((), ())), preferred_element_type=F32)


def _dot(a, b):
    return jnp.dot(a, b, preferred_element_type=F32)


def _layernorm(x, g, b):
    mu = jnp.mean(x, axis=-1, keepdims=True)
    xc = x - mu
    var = jnp.mean(xc * xc, axis=-1, keepdims=True)
    return xc * lax.rsqrt(var + LN_EPS) * g + b


def _silu(x):
    return x * jax.nn.sigmoid(x)


def _params(semantics):
    return pltpu.CompilerParams(dimension_semantics=semantics,
                                vmem_limit_bytes=VMEM_LIMIT_BYTES)


def _rope_lanes(z, cos, sin):
    lane = lax.broadcasted_iota(jnp.int32, z.shape, 1)
    first_half = (lane % HEAD_DIM) < HALF_DIM
    partner = jnp.where(first_half, pltpu.roll(z, LANES - HALF_DIM, 1),
                        pltpu.roll(z, HALF_DIM, 1))
    return z * cos + partner * sin


def _qkv_prompt_kernel(x_ref, wq_ref, wkt_ref, wv_ref, cos_ref, sin_ref, cost_ref, sint_ref,
                       kbuf_ref, vbuf_ref, qb_ref, ktb_ref, vb_ref, kt_ref, v_ref):
    del kbuf_ref, vbuf_ref
    xb = x_ref[...].astype(BF16)
    cos = cos_ref[...]
    sin = sin_ref[...]
    q = _dot(xb, wq_ref[...])
    for c in range(ATTN_WIDTH // LANES):
        sl = slice(c * LANES, (c + 1) * LANES)
        qb_ref[:, sl] = (_rope_lanes(q[:, sl], cos, sin) * SCALE).astype(BF16)

    kt = _nt_dot(wkt_ref[...], xb)
    cost = cost_ref[...]
    sint = sint_ref[...]
    for j in range(2 * N_HEADS):
        x1 = kt[j * HEAD_DIM:j * HEAD_DIM + HALF_DIM]
        x2 = kt[j * HEAD_DIM + HALF_DIM:(j + 1) * HEAD_DIM]
        o1 = x1 * cost - x2 * sint
        o2 = x2 * cost + x1 * sint
        kt_ref[j, :HALF_DIM, :] = o1
        kt_ref[j, HALF_DIM:, :] = o2
        ktb_ref[j * HEAD_DIM:j * HEAD_DIM + HALF_DIM, :] = o1.astype(BF16)
        ktb_ref[j * HEAD_DIM + HALF_DIM:(j + 1) * HEAD_DIM, :] = o2.astype(BF16)

    v = _dot(xb, wv_ref[...])
    rows = v.shape[0]
    ones = jnp.ones((rows, VALUE_DIM), BF16)
    for h in range(N_HEADS):
        vh = v[:, h * LANES:(h + 1) * LANES]
        v_ref[pl.ds(h, rows, stride=N_HEADS), :] = vh
        vb_ref[:, h * VEXT_WIDTH:h * VEXT_WIDTH + VALUE_DIM] = vh.astype(BF16)
        vb_ref[:, h * VEXT_WIDTH + VALUE_DIM:(h + 1) * VEXT_WIDTH] = ones


def _qkv_prompt_call(l, x, w_in_b, wkt_b, cos_t, sin_t, cost_t, sint_t, kbuf, vbuf):
    n = x.shape[0]
    tm = ROW_TILE
    row = lambda i: (i, 0)
    col = lambda i: (0, i)
    any_spec = pl.BlockSpec(memory_space=pl.ANY)
    return pl.pallas_call(
        _qkv_prompt_kernel,
        grid=(n // tm,),
        in_specs=[
            pl.BlockSpec((tm, D_MODEL), row),
            pl.BlockSpec((None, D_MODEL, ATTN_WIDTH), lambda i: (l, 0, 0)),
            pl.BlockSpec((None, ATTN_WIDTH, D_MODEL), lambda i: (l, 0, 0)),
            pl.BlockSpec((None, D_MODEL, ATTN_WIDTH), lambda i: (l, 0, 2)),
            pl.BlockSpec((tm, LANES), row),
            pl.BlockSpec((tm, LANES), row),
            pl.BlockSpec((HALF_DIM, tm), col),
            pl.BlockSpec((HALF_DIM, tm), col),
            any_spec, any_spec,
        ],
        out_specs=[
            pl.BlockSpec((tm, ATTN_WIDTH), row),
            pl.BlockSpec((ATTN_WIDTH, tm), col),
            pl.BlockSpec((tm, N_HEADS * VEXT_WIDTH), row),
            pl.BlockSpec((None, 2 * N_HEADS, HEAD_DIM, tm), lambda i: (l, 0, 0, i)),
            pl.BlockSpec((None, tm * N_HEADS, VALUE_DIM), lambda i: (l, i, 0)),
        ],
        out_shape=[
            jax.ShapeDtypeStruct((n, ATTN_WIDTH), BF16),
            jax.ShapeDtypeStruct((ATTN_WIDTH, n), BF16),
            jax.ShapeDtypeStruct((n, N_HEADS * VEXT_WIDTH), BF16),
            jax.ShapeDtypeStruct(kbuf.shape, F32),
            jax.ShapeDtypeStruct(vbuf.shape, F32),
        ],
        input_output_aliases={8: 3, 9: 4},
        compiler_params=_params(("parallel",)),
        name="qkv_rope_prompt",
    )(x, w_in_b, wkt_b, w_in_b, cos_t, sin_t, cost_t, sint_t, kbuf, vbuf)


def _qkv_sample_kernel(x_ref, w_ref, cos_ref, sin_ref, qb_ref, kb_ref, vb_ref, k_ref, v_ref):
    xb = x_ref[...].astype(BF16)
    h = _dot(xb, w_ref[...])
    cos = cos_ref[...]
    sin = sin_ref[...]
    for c in range(ATTN_WIDTH // LANES):
        sl = slice(c * LANES, (c + 1) * LANES)
        q = _rope_lanes(h[:, c * LANES:(c + 1) * LANES], cos, sin)
        qb_ref[:, sl] = (q * SCALE).astype(BF16)
        k = _rope_lanes(h[:, ATTN_WIDTH + c * LANES:ATTN_WIDTH + (c + 1) * LANES], cos, sin)
        k_ref[:, sl] = k
        kb_ref[:, sl] = k.astype(BF16)
    v = h[:, 2 * ATTN_WIDTH:3 * ATTN_WIDTH]
    v_ref[...] = v
    vb_ref[...] = v.astype(BF16)


def _qkv_sample_call(l, x, w_in_b, cos_t, sin_t):
    n = x.shape[0]
    tm = ROW_TILE
    row = lambda i: (i, 0)
    return pl.pallas_call(
        _qkv_sample_kernel,
        grid=(n // tm,),
        in_specs=[
            pl.BlockSpec((tm, D_MODEL), row),
            pl.BlockSpec((None, D_MODEL, QKV_WIDTH), lambda i: (l, 0, 0)),
            pl.BlockSpec((tm, LANES), row),
            pl.BlockSpec((tm, LANES), row),
        ],
        out_specs=[pl.BlockSpec((tm, ATTN_WIDTH), row)] * 5,
        out_shape=[jax.ShapeDtypeStruct((n, ATTN_WIDTH), BF16)] * 3
        + [jax.ShapeDtypeStruct((n, ATTN_WIDTH), F32)] * 2,
        compiler_params=_params(("parallel",)),
        name="qkv_rope_sample",
    )(x, w_in_b, cos_t, sin_t)


def _lambda_value(lq1_ref, lk1_ref, lq2_ref, lk2_ref, lam_init):
    a = jnp.sum(lq1_ref[...] * lk1_ref[...], axis=-1, keepdims=True)
    b = jnp.sum(lq2_ref[...] * lk2_ref[...], axis=-1, keepdims=True)
    return jnp.exp(a) - jnp.exp(b) + lam_init


def _stack_sub_queries(q_ref, qs_ref, rows):
    lane = lax.broadcasted_iota(jnp.int32, (rows, LANES), 1)
    low = lane < HEAD_DIM
    zero = jnp.zeros((rows, LANES), BF16)
    for h in range(N_HEADS):
        qh = q_ref[:, h * LANES:(h + 1) * LANES]
        qs_ref[h, :rows, :] = jnp.where(low, qh, zero)
        qs_ref[h, rows:, :] = jnp.where(low, zero, qh)


def _combine_heads(acc0, acc1, l0, l1, lam, subw, lam_init):
    o = acc0 / l0 - lam * (acc1 / l1)
    ms = jnp.mean(o * o, axis=-1, keepdims=True)
    return o * lax.rsqrt(ms + LN_EPS) * subw * (1.0 - lam_init)


def _attn_prompt_kernel(qi_ref, ki_ref, q_ref, kt_ref, v_ref, lq1_ref, lk1_ref, lq2_ref,
                        lk2_ref, subw_ref, o_ref, qs_ref, m_ref, acc_ref, *,
                        tile, ktile, chain, lam_init):
    ratio = ktile // tile
    step = pl.program_id(0)
    qi = qi_ref[step]
    ki = ki_ref[step]
    last = qi // ratio

    @pl.when(ki == 0)
    def _():
        _stack_sub_queries(q_ref, qs_ref, tile)
        m_ref[...] = jnp.full(m_ref.shape, -jnp.inf, F32)
        acc_ref[...] = jnp.zeros(acc_ref.shape, F32)

    def chain_step(h, sub, r0, nrows, k0, k1, masked):
        width = k1 - k0
        rows = slice(sub * tile + r0, sub * tile + r0 + nrows)
        s = _dot(qs_ref[h, rows], kt_ref[h * LANES:(h + 1) * LANES, k0:k1])
        if masked:
            r = lax.broadcasted_iota(jnp.int32, (nrows, width), 0)
            c = lax.broadcasted_iota(jnp.int32, (nrows, width), 1)
            s = jnp.where((c // CHUNK) <= (r // CHUNK), s, -jnp.inf)
        m_prev = m_ref[h, rows]
        m_next = jnp.maximum(m_prev, jnp.max(s, axis=-1, keepdims=True))
        alpha = jnp.exp(m_prev - m_next)
        p = jnp.exp((s - pltpu.repeat(m_next, width // LANES, axis=1)).astype(BF16))
        m_ref[h, rows] = m_next
        pv = _dot(p, v_ref[k0:k1, h * VEXT_WIDTH:(h + 1) * VEXT_WIDTH])
        acc_ref[h, rows] = pltpu.repeat(alpha, VEXT_WIDTH // LANES, axis=1) * acc_ref[h, rows] + pv

    def update(r0, k0, k1, masked):
        for h in range(N_HEADS):
            for sub in range(2):
                chain_step(h, sub, r0, chain, k0, k1, masked)

    @pl.when(ki < last)
    def _():
        maps = [(h, sub) for h in range(N_HEADS) for sub in range(2)]
        blocks = [(h, sub, rb * chain, chain) for rb in range(tile // chain) for h, sub in maps]
        halve = lambda blk: [(blk[0], blk[1], blk[2], blk[3] // 2),
                             (blk[0], blk[1], blk[2] + blk[3] // 2, blk[3] // 2)]
        blocks = (halve(blocks[0]) + halve(blocks[1]) + blocks[2:-2]
                  + halve(blocks[-2]) + halve(blocks[-1]))
        for h, sub, r0, nrows in blocks:
            chain_step(h, sub, r0, nrows, 0, ktile, False)

    for part in range(ratio):
        @pl.when((ki == last) & (qi % ratio == part))
        def _():
            for rb in range(tile // chain):
                diag = part * tile + rb * chain
                if diag:
                    update(rb * chain, 0, diag, False)
                update(rb * chain, diag, diag + chain, True)
            lam = _lambda_value(lq1_ref, lk1_ref, lq2_ref, lk2_ref, lam_init)
            subw = subw_ref[...]
            for h in range(N_HEADS):
                acc = acc_ref[h]
                o_ref[:, h * LANES:(h + 1) * LANES] = _combine_heads(
                    acc[:tile, :VALUE_DIM], acc[tile:, :VALUE_DIM],
                    acc[:tile, VALUE_DIM:], acc[tile:, VALUE_DIM:], lam, subw, lam_init)


def _attn_prompt_call(l, qb, ktb, vb, lam_vecs, subw, lam_init):
    s = qb.shape[0]
    t, kt = ATTN_TILE, ATTN_K_TILE
    ratio = kt // t
    nq = s // t
    qi = np.concatenate([np.full(i // ratio + 1, i, np.int32) for i in range(nq)])
    ki = np.concatenate([np.arange(i // ratio + 1, dtype=np.int32) for i in range(nq)])
    vec = pl.BlockSpec((None, 1, HEAD_DIM), lambda g, qi, ki: (l, 0, 0))
    grid_spec = pltpu.PrefetchScalarGridSpec(
        num_scalar_prefetch=2,
        grid=(len(qi),),
        in_specs=[
            pl.BlockSpec((t, ATTN_WIDTH), lambda g, qi, ki: (qi[g], 0)),
            pl.BlockSpec((ATTN_WIDTH, kt), lambda g, qi, ki: (0, ki[g])),
            pl.BlockSpec((kt, N_HEADS * VEXT_WIDTH), lambda g, qi, ki: (ki[g], 0)),
            vec, vec, vec, vec,
            pl.BlockSpec((None, 1, VALUE_DIM), lambda g, qi, ki: (l, 0, 0)),
        ],
        out_specs=pl.BlockSpec((t, ATTN_WIDTH), lambda g, qi, ki: (qi[g], 0)),
        scratch_shapes=[
            pltpu.VMEM((N_HEADS, 2 * t, LANES), BF16),
            pltpu.VMEM((N_HEADS, 2 * t, LANES), F32),
            pltpu.VMEM((N_HEADS, 2 * t, VEXT_WIDTH), F32),
        ],
    )
    return pl.pallas_call(
        functools.partial(_attn_prompt_kernel, tile=t, ktile=kt, chain=ATTN_CHAIN,
                          lam_init=lam_init),
        grid_spec=grid_spec,
        out_shape=jax.ShapeDtypeStruct((s, ATTN_WIDTH), F32),
        compiler_params=_params(("arbitrary",)),
        name="diff_attn_prompt",
    )(jnp.asarray(qi), jnp.asarray(ki), qb, ktb, vb, *lam_vecs, subw)


def _attn_sample_kernel(q_ref, k_ref, v_ref, ckt_ref, cv_ref, lq1_ref, lk1_ref, lq2_ref,
                        lk2_ref, subw_ref, o_ref, qs_ref, *, rows, lam_init):
    _stack_sub_queries(q_ref, qs_ref, rows)
    lam = _lambda_value(lq1_ref, lk1_ref, lq2_ref, lk2_ref, lam_init)
    subw = subw_ref[...]
    past = ckt_ref.shape[-1]
    for h in range(N_HEADS):
        hs = slice(h * LANES, (h + 1) * LANES)
        qs = qs_ref[h]
        ckt = ckt_ref[2 * h:2 * h + 2].reshape(LANES, past).astype(BF16)
        s_c = _dot(qs, ckt)
        s_n = _nt_dot(qs, k_ref[:, hs])
        m = jnp.maximum(jnp.max(s_c, axis=-1, keepdims=True),
                        jnp.max(s_n, axis=-1, keepdims=True))
        p_c = jnp.exp(s_c - m)
        p_n = jnp.exp(s_n - m)
        l = jnp.sum(p_c, axis=-1, keepdims=True) + jnp.sum(p_n, axis=-1, keepdims=True)
        cv_h = cv_ref[pl.ds(h, past, stride=N_HEADS), :]
        acc = _dot(p_c.astype(BF16), cv_h.astype(BF16))
        acc = acc + _dot(p_n.astype(BF16), v_ref[:, hs])
        o_ref[:, hs] = _combine_heads(acc[:rows], acc[rows:], l[:rows], l[rows:], lam, subw,
                                      lam_init)


def _attn_sample_call(l, qb, kb, vb, ckt, cv, lam_vecs, subw, lam_init, rows):
    n = qb.shape[0] // rows
    past = cv.shape[2] // N_HEADS
    vec = pl.BlockSpec((None, 1, HEAD_DIM), lambda i: (l, 0, 0))
    new = pl.BlockSpec((rows, ATTN_WIDTH), lambda i: (i, 0))
    return pl.pallas_call(
        functools.partial(_attn_sample_kernel, rows=rows, lam_init=lam_init),
        grid=(n,),
        in_specs=[
            new, new, new,
            pl.BlockSpec((None, None, 2 * N_HEADS, HEAD_DIM, past), lambda i: (l, i, 0, 0, 0)),
            pl.BlockSpec((None, None, past * N_HEADS, VALUE_DIM), lambda i: (l, i, 0, 0)),
            vec, vec, vec, vec,
            pl.BlockSpec((None, 1, VALUE_DIM), lambda i: (l, 0, 0)),
        ],
        out_specs=new,
        out_shape=jax.ShapeDtypeStruct((n * rows, ATTN_WIDTH), F32),
        scratch_shapes=[pltpu.VMEM((N_HEADS, 2 * rows, LANES), BF16)],
        compiler_params=_params(("parallel",)),
        name="diff_attn_sample",
    )(qb, kb, vb, ckt, cv, *lam_vecs, subw)


_GA = (QKV_WIDTH, QKV_WIDTH + ATTN_WIDTH)
_U = (_GA[1], _GA[1] + GMLP_WIDTH)
_VG = (_U[1], _U[1] + GMLP_WIDTH)
_GG = (_VG[1], _VG[1] + GMLP_WIDTH)
_MA = (_GG[1], _GG[1] + D_MODEL)
_MB = (_MA[1], IN_WIDTH)


def _post_kernel(x_ref, o_ref, w_ref, woa_ref, wog_ref, wout_ref, sg_ref, sb_ref, ws_ref,
                 bs_ref, lng_ref, lnb_ref, *out_and_scratch, period, emit_vn):
    if emit_vn:
        xo_ref, vn_ref, sgu_ref = out_and_scratch
    else:
        xo_ref, sgu_ref = out_and_scratch
    tm = x_ref.shape[0]
    x = x_ref[...]
    xb = x.astype(BF16)

    def proj(cols):
        return _dot(xb, w_ref[:, cols[0]:cols[1]])

    ta = (o_ref[...] * _silu(proj(_GA))).astype(BF16)
    ya = _dot(ta, woa_ref[...])

    vn = _layernorm(proj(_VG), sg_ref[...], sb_ref[...])
    if emit_vn:
        vn_ref[...] = vn
    vnb = vn.astype(BF16)
    r = lax.broadcasted_iota(jnp.int32, (GMLP_CHUNK, GMLP_CHUNK), 0)
    c = lax.broadcasted_iota(jnp.int32, (GMLP_CHUNK, GMLP_CHUNK), 1)
    causal = (c <= r) & ((r // period) == (c // period))
    bias = bs_ref[...]
    for g in range(GMLP_GROUPS):
        gs = slice(g * LANES, (g + 1) * LANES)
        wsg = jnp.where(causal, ws_ref[g], 0.0).astype(BF16)
        for ch in range(tm // GMLP_CHUNK):
            rs = slice(ch * GMLP_CHUNK, (ch + 1) * GMLP_CHUNK)
            sgu_ref[rs, gs] = _dot(wsg, vnb[rs, gs]) + bias[:, gs]
    tg = (proj(_U) * sgu_ref[...] * _silu(proj(_GG))).astype(BF16)
    yg = _dot(tg, wog_ref[...])

    merged = jax.nn.sigmoid(proj(_MA)) * ya + jax.nn.sigmoid(proj(_MB)) * yg
    y = _dot(merged.astype(BF16), wout_ref[...])
    xo_ref[...] = _layernorm(ALPHA * x + y, lng_ref[...], lnb_ref[...])


def _post_call(l, x, o, w_in_b, woa_b, wog_b, wout_b, sg, sb, ws, bs, lng, lnb, period,
               emit_vn):
    n = x.shape[0]
    tm = POST_TILE
    row = lambda i: (i, 0)
    layer3 = lambda i: (l, 0, 0)
    out_specs = [pl.BlockSpec((tm, D_MODEL), row)]
    out_shape = [jax.ShapeDtypeStruct((n, D_MODEL), F32)]
    if emit_vn:
        out_specs.append(pl.BlockSpec((tm, GMLP_WIDTH), row))
        out_shape.append(jax.ShapeDtypeStruct((n, GMLP_WIDTH), F32))
    return pl.pallas_call(
        functools.partial(_post_kernel, period=period, emit_vn=emit_vn),
        grid=(n // tm,),
        in_specs=[
            pl.BlockSpec((tm, D_MODEL), row),
            pl.BlockSpec((tm, ATTN_WIDTH), row),
            pl.BlockSpec((None, D_MODEL, IN_WIDTH), layer3, pipeline_mode=pl.Buffered(1)),
            pl.BlockSpec((None, ATTN_WIDTH, D_MODEL), layer3, pipeline_mode=pl.Buffered(1)),
            pl.BlockSpec((None, GMLP_WIDTH, D_MODEL), layer3, pipeline_mode=pl.Buffered(1)),
            pl.BlockSpec((None, D_MODEL, D_MODEL), layer3, pipeline_mode=pl.Buffered(1)),
            pl.BlockSpec((None, 1, GMLP_WIDTH), layer3),
            pl.BlockSpec((None, 1, GMLP_WIDTH), layer3),
            pl.BlockSpec((None, GMLP_GROUPS, GMLP_CHUNK, GMLP_CHUNK), lambda i: (l, 0, 0, 0)),
            pl.BlockSpec((None, GMLP_CHUNK, GMLP_WIDTH), layer3),
            pl.BlockSpec((None, 1, D_MODEL), layer3),
            pl.BlockSpec((None, 1, D_MODEL), layer3),
        ],
        out_specs=out_specs,
        out_shape=out_shape,
        scratch_shapes=[pltpu.VMEM((tm, GMLP_WIDTH), F32)],
        compiler_params=_params(("parallel",)),
        name="post_mix",
    )(x, o, w_in_b, woa_b, wog_b, wout_b, sg, sb, ws, bs, lng, lnb)


def _rope_angles(pos):
    inv = ROPE_THETA ** (-jnp.arange(HALF_DIM, dtype=F32) / HALF_DIM)
    ang = pos.astype(F32)[:, None] * inv[None, :]
    return jnp.cos(ang), jnp.sin(ang)


def _lane_tables(cos, sin):
    reps = LANES // HEAD_DIM
    cos_t = jnp.tile(jnp.concatenate([cos, cos], -1), (1, reps))
    sin_t = jnp.tile(jnp.concatenate([-sin, sin], -1), (1, reps))
    return cos_t, sin_t


def kernel(x_prompt, x_sample, cache_k, cache_v, w_in, w_oa, w_og, w_out, lambda_q1,
           lambda_k1, lambda_q2, lambda_k2, subln_w, sgu_ln_g, sgu_ln_b, w_s, b_s, ln_g, ln_b):
    batch, seq, _ = x_prompt.shape
    dec_batch, dec_seq, _ = x_sample.shape
    assert batch == 1 and seq % ATTN_K_TILE == 0 and seq % ROW_TILE == 0
    assert (dec_batch * dec_seq) % ROW_TILE == 0 and GMLP_CHUNK % dec_seq == 0

    cos_p, sin_p = _rope_angles(jnp.arange(seq))
    cos_pl, sin_pl = _lane_tables(cos_p, sin_p)
    cos_pt, sin_pt = cos_p.T, sin_p.T
    cos_s, sin_s = _lane_tables(*_rope_angles(PAST_LEN + jnp.arange(dec_seq)))
    cos_s = jnp.tile(cos_s, (dec_batch, 1))
    sin_s = jnp.tile(sin_s, (dec_batch, 1))

    xp = x_prompt.reshape(seq, D_MODEL)
    xs = x_sample.reshape(dec_batch * dec_seq, D_MODEL)
    ckt = jnp.transpose(cache_k, (0, 1, 3, 4, 2))
    cv = cache_v.reshape(DEPTH, dec_batch, PAST_LEN * N_HEADS, VALUE_DIM)

    w_in_b = w_in.astype(BF16)
    wkt_b = jnp.transpose(w_in[:, :, ATTN_WIDTH:2 * ATTN_WIDTH], (0, 2, 1)).astype(BF16)
    woa_b = w_oa.astype(BF16)
    wog_b = w_og.astype(BF16)
    wout_b = w_out.astype(BF16)
    lam_vecs = tuple(v.reshape(DEPTH, 1, HEAD_DIM)
                     for v in (lambda_q1, lambda_k1, lambda_q2, lambda_k2))
    subw = subln_w.reshape(DEPTH, 1, VALUE_DIM)
    sg = sgu_ln_g.reshape(DEPTH, 1, GMLP_WIDTH)
    sb = sgu_ln_b.reshape(DEPTH, 1, GMLP_WIDTH)
    lng = ln_g.reshape(DEPTH, 1, D_MODEL)
    lnb = ln_b.reshape(DEPTH, 1, D_MODEL)
    bias_p = jnp.repeat(jnp.transpose(b_s, (0, 2, 1)), GMLP_WIDTH // GMLP_GROUPS, axis=2)
    reps = GMLP_CHUNK // dec_seq
    ws_s = jnp.tile(w_s[:, :, :dec_seq, :dec_seq], (1, 1, reps, reps))
    bias_s = jnp.tile(bias_p[:, :dec_seq], (1, reps, 1))

    kbuf = jnp.zeros((DEPTH, 2 * N_HEADS, HEAD_DIM, seq), F32)
    vbuf = jnp.zeros((DEPTH, seq * N_HEADS, VALUE_DIM), F32)

    ks_l, vs_l, gs_l = [], [], []
    for l in range(DEPTH):
        lam_init = 0.8 - 0.6 * math.exp(-0.3 * l)

        qb, ktb, vb, kbuf, vbuf = _qkv_prompt_call(l, xp, w_in_b, wkt_b, cos_pl, sin_pl,
                                                   cos_pt, sin_pt, kbuf, vbuf)
        op = _attn_prompt_call(l, qb, ktb, vb, lam_vecs, subw, lam_init)
        (xp,) = _post_call(l, xp, op, w_in_b, woa_b, wog_b, wout_b, sg, sb, w_s, bias_p,
                           lng, lnb, GMLP_CHUNK, False)

        qb, kb, vb, ksn, vsn = _qkv_sample_call(l, xs, w_in_b, cos_s, sin_s)
        osn = _attn_sample_call(l, qb, kb, vb, ckt, cv, lam_vecs, subw, lam_init, dec_seq)
        xs, gsn = _post_call(l, xs, osn, w_in_b, woa_b, wog_b, wout_b, sg, sb, ws_s, bias_s,
                             lng, lnb, dec_seq, True)
        ks_l.append(ksn); vs_l.append(vsn); gs_l.append(gsn)

    new_k_prompt = jnp.transpose(kbuf.reshape(DEPTH, batch, 2 * N_HEADS, HEAD_DIM, seq),
                                 (0, 1, 4, 2, 3))
    new_v_prompt = vbuf.reshape(DEPTH, batch, seq, N_HEADS, VALUE_DIM)
    new_k_sample = jnp.stack(ks_l).reshape(DEPTH, dec_batch, dec_seq, 2 * N_HEADS, HEAD_DIM)
    new_v_sample = jnp.stack(vs_l).reshape(DEPTH, dec_batch, dec_seq, N_HEADS, VALUE_DIM)
    new_gv_sample = jnp.stack(gs_l).reshape(DEPTH, dec_batch, dec_seq, GMLP_WIDTH)
    return (xp.reshape(batch, seq, D_MODEL), xs.reshape(dec_batch, dec_seq, D_MODEL),
            new_k_prompt, new_v_prompt, new_k_sample, new_v_sample, new_gv_sample)
```

```python
import functools
import math

import jax
import jax.numpy as jnp
import numpy as np
from jax import lax
from jax.experimental import pallas as pl
from jax.experimental.pallas import tpu as pltpu

D_MODEL = 1024
DEPTH = 4
PAST_LEN = 1024
CHUNK = 64
N_HEADS = 4
HEAD_DIM = 64
HALF_DIM = HEAD_DIM // 2
VALUE_DIM = 2 * HEAD_DIM
ATTN_WIDTH = N_HEADS * VALUE_DIM
GMLP_CHUNK = 128
GMLP_GROUPS = 4
GMLP_WIDTH = 512
ROPE_THETA = 10000.0
ALPHA = (2 * DEPTH) ** 0.25
LN_EPS = 1e-5
SCALE = HEAD_DIM ** -0.5
QKV_WIDTH = 3 * ATTN_WIDTH
IN_WIDTH = QKV_WIDTH + ATTN_WIDTH + 3 * GMLP_WIDTH + 2 * D_MODEL

LANES = 128
VMEM_LIMIT_BYTES = 56 * 1024 * 1024

ROW_TILE = 1024
POST_TILE = 1024
ATTN_TILE = 1024
ATTN_K_TILE = 1024
ATTN_CHAIN = 512
VEXT_WIDTH = 2 * VALUE_DIM

F32 = jnp.float32
BF16 = jnp.bfloat16


def _nt_dot(a, b):
    return lax.dot_general(a, b, (((1,), (1,)), ((), ())), preferred_element_type=F32)


def _dot(a, b):
    return jnp.dot(a, b, preferred_element_type=F32)


def _layernorm(x, g, b):
    mu = jnp.mean(x, axis=-1, keepdims=True)
    xc = x - mu
    var = jnp.mean(xc * xc, axis=-1, keepdims=True)
    return xc * lax.rsqrt(var + LN_EPS) * g + b


def _silu(x):
    return x * jax.nn.sigmoid(x)


def _params(semantics):
    return pltpu.CompilerParams(dimension_semantics=semantics,
                                vmem_limit_bytes=VMEM_LIMIT_BYTES)


def _rope_lanes(z, cos, sin):
    lane = lax.broadcasted_iota(jnp.int32, z.shape, 1)
    first_half = (lane % HEAD_DIM) < HALF_DIM
    partner = jnp.where(first_half, pltpu.roll(z, LANES - HALF_DIM, 1),
                        pltpu.roll(z, HALF_DIM, 1))
    return z * cos + partner * sin


def _qkv_prompt_kernel(x_ref, wq_ref, wkt_ref, wv_ref, cos_ref, sin_ref, cost_ref, sint_ref,
                       kbuf_ref, vbuf_ref, qb_ref, ktb_ref, vb_ref, kt_ref, v_ref):
    del kbuf_ref, vbuf_ref
    xb = x_ref[...].astype(BF16)
    cos = cos_ref[...]
    sin = sin_ref[...]
    q = _dot(xb, wq_ref[...])
    for c in range(ATTN_WIDTH // LANES):
        sl = slice(c * LANES, (c + 1) * LANES)
        qb_ref[:, sl] = (_rope_lanes(q[:, sl], cos, sin) * SCALE).astype(BF16)

    kt = _nt_dot(wkt_ref[...], xb)
    cost = cost_ref[...]
    sint = sint_ref[...]
    for j in range(2 * N_HEADS):
        x1 = kt[j * HEAD_DIM:j * HEAD_DIM + HALF_DIM]
        x2 = kt[j * HEAD_DIM + HALF_DIM:(j + 1) * HEAD_DIM]
        o1 = x1 * cost - x2 * sint
        o2 = x2 * cost + x1 * sint
        kt_ref[j, :HALF_DIM, :] = o1
        kt_ref[j, HALF_DIM:, :] = o2
        ktb_ref[j * HEAD_DIM:j * HEAD_DIM + HALF_DIM, :] = o1.astype(BF16)
        ktb_ref[j * HEAD_DIM + HALF_DIM:(j + 1) * HEAD_DIM, :] = o2.astype(BF16)

    v = _dot(xb, wv_ref[...])
    rows = v.shape[0]
    ones = jnp.ones((rows, VALUE_DIM), BF16)
    for h in range(N_HEADS):
        vh = v[:, h * LANES:(h + 1) * LANES]
        v_ref[pl.ds(h, rows, stride=N_HEADS), :] = vh
        vb_ref[:, h * VEXT_WIDTH:h * VEXT_WIDTH + VALUE_DIM] = vh.astype(BF16)
        vb_ref[:, h * VEXT_WIDTH + VALUE_DIM:(h + 1) * VEXT_WIDTH] = ones


def _qkv_prompt_call(l, x, w_in_b, wkt_b, cos_t, sin_t, cost_t, sint_t, kbuf, vbuf):
    n = x.shape[0]
    tm = ROW_TILE
    row = lambda i: (i, 0)
    col = lambda i: (0, i)
    any_spec = pl.BlockSpec(memory_space=pl.ANY)
    return pl.pallas_call(
        _qkv_prompt_kernel,
        grid=(n // tm,),
        in_specs=[
            pl.BlockSpec((tm, D_MODEL), row),
            pl.BlockSpec((None, D_MODEL, ATTN_WIDTH), lambda i: (l, 0, 0)),
            pl.BlockSpec((None, ATTN_WIDTH, D_MODEL), lambda i: (l, 0, 0)),
            pl.BlockSpec((None, D_MODEL, ATTN_WIDTH), lambda i: (l, 0, 2)),
            pl.BlockSpec((tm, LANES), row),
            pl.BlockSpec((tm, LANES), row),
            pl.BlockSpec((HALF_DIM, tm), col),
            pl.BlockSpec((HALF_DIM, tm), col),
            any_spec, any_spec,
        ],
        out_specs=[
            pl.BlockSpec((tm, ATTN_WIDTH), row),
            pl.BlockSpec((ATTN_WIDTH, tm), col),
            pl.BlockSpec((tm, N_HEADS * VEXT_WIDTH), row),
            pl.BlockSpec((None, 2 * N_HEADS, HEAD_DIM, tm), lambda i: (l, 0, 0, i)),
            pl.BlockSpec((None, tm * N_HEADS, VALUE_DIM), lambda i: (l, i, 0)),
        ],
        out_shape=[
            jax.ShapeDtypeStruct((n, ATTN_WIDTH), BF16),
            jax.ShapeDtypeStruct((ATTN_WIDTH, n), BF16),
            jax.ShapeDtypeStruct((n, N_HEADS * VEXT_WIDTH), BF16),
            jax.ShapeDtypeStruct(kbuf.shape, F32),
            jax.ShapeDtypeStruct(vbuf.shape, F32),
        ],
        input_output_aliases={8: 3, 9: 4},
        compiler_params=_params(("parallel",)),
        name="qkv_rope_prompt",
    )(x, w_in_b, wkt_b, w_in_b, cos_t, sin_t, cost_t, sint_t, kbuf, vbuf)


def _qkv_sample_kernel(x_ref, w_ref, cos_ref, sin_ref, qb_ref, kb_ref, vb_ref, k_ref, v_ref):
    xb = x_ref[...].astype(BF16)
    h = _dot(xb, w_ref[...])
    cos = cos_ref[...]
    sin = sin_ref[...]
    for c in range(ATTN_WIDTH // LANES):
        sl = slice(c * LANES, (c + 1) * LANES)
        q = _rope_lanes(h[:, c * LANES:(c + 1) * LANES], cos, sin)
        qb_ref[:, sl] = (q * SCALE).astype(BF16)
        k = _rope_lanes(h[:, ATTN_WIDTH + c * LANES:ATTN_WIDTH + (c + 1) * LANES], cos, sin)
        k_ref[:, sl] = k
        kb_ref[:, sl] = k.astype(BF16)
    v = h[:, 2 * ATTN_WIDTH:3 * ATTN_WIDTH]
    v_ref[...] = v
    vb_ref[...] = v.astype(BF16)


def _qkv_sample_call(l, x, w_in_b, cos_t, sin_t):
    n = x.shape[0]
    tm = ROW_TILE
    row = lambda i: (i, 0)
    return pl.pallas_call(
        _qkv_sample_kernel,
        grid=(n // tm,),
        in_specs=[
            pl.BlockSpec((tm, D_MODEL), row),
            pl.BlockSpec((None, D_MODEL, QKV_WIDTH), lambda i: (l, 0, 0)),
            pl.BlockSpec((tm, LANES), row),
            pl.BlockSpec((tm, LANES), row),
        ],
        out_specs=[pl.BlockSpec((tm, ATTN_WIDTH), row)] * 5,
        out_shape=[jax.ShapeDtypeStruct((n, ATTN_WIDTH), BF16)] * 3
        + [jax.ShapeDtypeStruct((n, ATTN_WIDTH), F32)] * 2,
        compiler_params=_params(("parallel",)),
        name="qkv_rope_sample",
    )(x, w_in_b, cos_t, sin_t)


def _lambda_value(lq1_ref, lk1_ref, lq2_ref, lk2_ref, lam_init):
    a = jnp.sum(lq1_ref[...] * lk1_ref[...], axis=-1, keepdims=True)
    b = jnp.sum(lq2_ref[...] * lk2_ref[...], axis=-1, keepdims=True)
    return jnp.exp(a) - jnp.exp(b) + lam_init


def _stack_sub_queries(q_ref, qs_ref, rows):
    lane = lax.broadcasted_iota(jnp.int32, (rows, LANES), 1)
    low = lane < HEAD_DIM
    zero = jnp.zeros((rows, LANES), BF16)
    for h in range(N_HEADS):
        qh = q_ref[:, h * LANES:(h + 1) * LANES]
        qs_ref[h, :rows, :] = jnp.where(low, qh, zero)
        qs_ref[h, rows:, :] = jnp.where(low, zero, qh)


def _combine_heads(acc0, acc1, l0, l1, lam, subw, lam_init):
    o = acc0 / l0 - lam * (acc1 / l1)
    ms = jnp.mean(o * o, axis=-1, keepdims=True)
    return o * lax.rsqrt(ms + LN_EPS) * subw * (1.0 - lam_init)


def _attn_prompt_kernel(qi_ref, ki_ref, q_ref, kt_ref, v_ref, lq1_ref, lk1_ref, lq2_ref,
                        lk2_ref, subw_ref, o_ref, qs_ref, m_ref, acc_ref, *,
                        tile, ktile, chain, lam_init):
    ratio = ktile // tile
    step = pl.program_id(0)
    qi = qi_ref[step]
    ki = ki_ref[step]
    last = qi // ratio

    @pl.when(ki == 0)
    def _():
        _stack_sub_queries(q_ref, qs_ref, tile)
        m_ref[...] = jnp.full(m_ref.shape, -jnp.inf, F32)
        acc_ref[...] = jnp.zeros(acc_ref.shape, F32)

    def chain_step(h, sub, r0, nrows, k0, k1, masked):
        width = k1 - k0
        rows = slice(sub * tile + r0, sub * tile + r0 + nrows)
        s = _dot(qs_ref[h, rows], kt_ref[h * LANES:(h + 1) * LANES, k0:k1])
        if masked:
            r = lax.broadcasted_iota(jnp.int32, (nrows, width), 0)
            c = lax.broadcasted_iota(jnp.int32, (nrows, width), 1)
            s = jnp.where((c // CHUNK) <= (r // CHUNK), s, -jnp.inf)
        m_prev = m_ref[h, rows]
        m_next = jnp.maximum(m_prev, jnp.max(s, axis=-1, keepdims=True))
        alpha = jnp.exp(m_prev - m_next)
        p = jnp.exp((s - pltpu.repeat(m_next, width // LANES, axis=1)).astype(BF16))
        m_ref[h, rows] = m_next
        pv = _dot(p, v_ref[k0:k1, h * VEXT_WIDTH:(h + 1) * VEXT_WIDTH])
        acc_ref[h, rows] = pltpu.repeat(alpha, VEXT_WIDTH // LANES, axis=1) * acc_ref[h, rows] + pv

    def update(r0, k0, k1, masked):
        for h in range(N_HEADS):
            for sub in range(2):
                chain_step(h, sub, r0, chain, k0, k1, masked)

    @pl.when(ki < last)
    def _():
        maps = [(h, sub) for h in range(N_HEADS) for sub in range(2)]
        blocks = [(h, sub, rb * chain, chain) for rb in range(tile // chain) for h, sub in maps]
        halve = lambda blk: [(blk[0], blk[1], blk[2], blk[3] // 2),
                             (blk[0], blk[1], blk[2] + blk[3] // 2, blk[3] // 2)]
        blocks = (halve(blocks[0]) + halve(blocks[1]) + blocks[2:-2]
                  + halve(blocks[-2]) + halve(blocks[-1]))
        for h, sub, r0, nrows in blocks:
            chain_step(h, sub, r0, nrows, 0, ktile, False)

    for part in range(ratio):
        @pl.when((ki == last) & (qi % ratio == part))
        def _():
            for rb in range(tile // chain):
                diag = part * tile + rb * chain
                if diag:
                    update(rb * chain, 0, diag, False)
                update(rb * chain, diag, diag + chain, True)
            lam = _lambda_value(lq1_ref, lk1_ref, lq2_ref, lk2_ref, lam_init)
            subw = subw_ref[...]
            for h in range(N_HEADS):
                acc = acc_ref[h]
                o_ref[:, h * LANES:(h + 1) * LANES] = _combine_heads(
                    acc[:tile, :VALUE_DIM], acc[tile:, :VALUE_DIM],
                    acc[:tile, VALUE_DIM:], acc[tile:, VALUE_DIM:], lam, subw, lam_init)


def _attn_prompt_call(l, qb, ktb, vb, lam_vecs, subw, lam_init):
    s = qb.shape[0]
    t, kt = ATTN_TILE, ATTN_K_TILE
    ratio = kt // t
    nq = s // t
    qi = np.concatenate([np.full(i // ratio + 1, i, np.int32) for i in range(nq)])
    ki = np.concatenate([np.arange(i // ratio + 1, dtype=np.int32) for i in range(nq)])
    vec = pl.BlockSpec((None, 1, HEAD_DIM), lambda g, qi, ki: (l, 0, 0))
    grid_spec = pltpu.PrefetchScalarGridSpec(
        num_scalar_prefetch=2,
        grid=(len(qi),),
        in_specs=[
            pl.BlockSpec((t, ATTN_WIDTH), lambda g, qi, ki: (qi[g], 0)),
            pl.BlockSpec((ATTN_WIDTH, kt), lambda g, qi, ki: (0, ki[g])),
            pl.BlockSpec((kt, N_HEADS * VEXT_WIDTH), lambda g, qi, ki: (ki[g], 0)),
            vec, vec, vec, vec,
            pl.BlockSpec((None, 1, VALUE_DIM), lambda g, qi, ki: (l, 0, 0)),
        ],
        out_specs=pl.BlockSpec((t, ATTN_WIDTH), lambda g, qi, ki: (qi[g], 0)),
        scratch_shapes=[
            pltpu.VMEM((N_HEADS, 2 * t, LANES), BF16),
            pltpu.VMEM((N_HEADS, 2 * t, LANES), F32),
            pltpu.VMEM((N_HEADS, 2 * t, VEXT_WIDTH), F32),
        ],
    )
    return pl.pallas_call(
        functools.partial(_attn_prompt_kernel, tile=t, ktile=kt, chain=ATTN_CHAIN,
                          lam_init=lam_init),
        grid_spec=grid_spec,
        out_shape=jax.ShapeDtypeStruct((s, ATTN_WIDTH), F32),
        compiler_params=_params(("arbitrary",)),
        name="diff_attn_prompt",
    )(jnp.asarray(qi), jnp.asarray(ki), qb, ktb, vb, *lam_vecs, subw)


def _attn_sample_kernel(q_ref, k_ref, v_ref, ckt_ref, cv_ref, lq1_ref, lk1_ref, lq2_ref,
                        lk2_ref, subw_ref, o_ref, qs_ref, *, rows, lam_init):
    _stack_sub_queries(q_ref, qs_ref, rows)
    lam = _lambda_value(lq1_ref, lk1_ref, lq2_ref, lk2_ref, lam_init)
    subw = subw_ref[...]
    past = ckt_ref.shape[-1]
    for h in range(N_HEADS):
        hs = slice(h * LANES, (h + 1) * LANES)
        qs = qs_ref[h]
        ckt = ckt_ref[2 * h:2 * h + 2].reshape(LANES, past).astype(BF16)
        s_c = _dot(qs, ckt)
        s_n = _nt_dot(qs, k_ref[:, hs])
        m = jnp.maximum(jnp.max(s_c, axis=-1, keepdims=True),
                        jnp.max(s_n, axis=-1, keepdims=True))
        p_c = jnp.exp(s_c - m)
        p_n = jnp.exp(s_n - m)
        l = jnp.sum(p_c, axis=-1, keepdims=True) + jnp.sum(p_n, axis=-1, keepdims=True)
        cv_h = cv_ref[pl.ds(h, past, stride=N_HEADS), :]
        acc = _dot(p_c.astype(BF16), cv_h.astype(BF16))
        acc = acc + _dot(p_n.astype(BF16), v_ref[:, hs])
        o_ref[:, hs] = _combine_heads(acc[:rows], acc[rows:], l[:rows], l[rows:], lam, subw,
                                      lam_init)


def _attn_sample_call(l, qb, kb, vb, ckt, cv, lam_vecs, subw, lam_init, rows):
    n = qb.shape[0] // rows
    past = cv.shape[2] // N_HEADS
    vec = pl.BlockSpec((None, 1, HEAD_DIM), lambda i: (l, 0, 0))
    new = pl.BlockSpec((rows, ATTN_WIDTH), lambda i: (i, 0))
    return pl.pallas_call(
        functools.partial(_attn_sample_kernel, rows=rows, lam_init=lam_init),
        grid=(n,),
        in_specs=[
            new, new, new,
            pl.BlockSpec((None, None, 2 * N_HEADS, HEAD_DIM, past), lambda i: (l, i, 0, 0, 0)),
            pl.BlockSpec((None, None, past * N_HEADS, VALUE_DIM), lambda i: (l, i, 0, 0)),
            vec, vec, vec, vec,
            pl.BlockSpec((None, 1, VALUE_DIM), lambda i: (l, 0, 0)),
        ],
        out_specs=new,
        out_shape=jax.ShapeDtypeStruct((n * rows, ATTN_WIDTH), F32),
        scratch_shapes=[pltpu.VMEM((N_HEADS, 2 * rows, LANES), BF16)],
        compiler_params=_params(("parallel",)),
        name="diff_attn_sample",
    )(qb, kb, vb, ckt, cv, *lam_vecs, subw)


_GA = (QKV_WIDTH, QKV_WIDTH + ATTN_WIDTH)
_U = (_GA[1], _GA[1] + GMLP_WIDTH)
_VG = (_U[1], _U[1] + GMLP_WIDTH)
_GG = (_VG[1], _VG[1] + GMLP_WIDTH)
_MA = (_GG[1], _GG[1] + D_MODEL)
_MB = (_MA[1], IN_WIDTH)


def _post_kernel(x_ref, o_ref, w_ref, woa_ref, wog_ref, wout_ref, sg_ref, sb_ref, ws_ref,
                 bs_ref, lng_ref, lnb_ref, *out_and_scratch, period, emit_vn):
    if emit_vn:
        xo_ref, vn_ref, sgu_ref = out_and_scratch
    else:
        xo_ref, sgu_ref = out_and_scratch
    tm = x_ref.shape[0]
    x = x_ref[...]
    xb = x.astype(BF16)

    def proj(cols):
        return _dot(xb, w_ref[:, cols[0]:cols[1]])

    ta = (o_ref[...] * _silu(proj(_GA))).astype(BF16)
    ya = _dot(ta, woa_ref[...])

    vn = _layernorm(proj(_VG), sg_ref[...], sb_ref[...])
    if emit_vn:
        vn_ref[...] = vn
    vnb = vn.astype(BF16)
    r = lax.broadcasted_iota(jnp.int32, (GMLP_CHUNK, GMLP_CHUNK), 0)
    c = lax.broadcasted_iota(jnp.int32, (GMLP_CHUNK, GMLP_CHUNK), 1)
    causal = (c <= r) & ((r // period) == (c // period))
    bias = bs_ref[...]
    for g in range(GMLP_GROUPS):
        gs = slice(g * LANES, (g + 1) * LANES)
        wsg = jnp.where(causal, ws_ref[g], 0.0).astype(BF16)
        for ch in range(tm // GMLP_CHUNK):
            rs = slice(ch * GMLP_CHUNK, (ch + 1) * GMLP_CHUNK)
            sgu_ref[rs, gs] = _dot(wsg, vnb[rs, gs]) + bias[:, gs]
    tg = (proj(_U) * sgu_ref[...] * _silu(proj(_GG))).astype(BF16)
    yg = _dot(tg, wog_ref[...])

    merged = jax.nn.sigmoid(proj(_MA)) * ya + jax.nn.sigmoid(proj(_MB)) * yg
    y = _dot(merged.astype(BF16), wout_ref[...])
    xo_ref[...] = _layernorm(ALPHA * x + y, lng_ref[...], lnb_ref[...])


def _post_call(l, x, o, w_in_b, woa_b, wog_b, wout_b, sg, sb, ws, bs, lng, lnb, period,
               emit_vn):
    n = x.shape[0]
    tm = POST_TILE
    row = lambda i: (i, 0)
    layer3 = lambda i: (l, 0, 0)
    out_specs = [pl.BlockSpec((tm, D_MODEL), row)]
    out_shape = [jax.ShapeDtypeStruct((n, D_MODEL), F32)]
    if emit_vn:
        out_specs.append(pl.BlockSpec((tm, GMLP_WIDTH), row))
        out_shape.append(jax.ShapeDtypeStruct((n, GMLP_WIDTH), F32))
    return pl.pallas_call(
        functools.partial(_post_kernel, period=period, emit_vn=emit_vn),
        grid=(n // tm,),
        in_specs=[
            pl.BlockSpec((tm, D_MODEL), row),
            pl.BlockSpec((tm, ATTN_WIDTH), row),
            pl.BlockSpec((None, D_MODEL, IN_WIDTH), layer3, pipeline_mode=pl.Buffered(1)),
            pl.BlockSpec((None, ATTN_WIDTH, D_MODEL), layer3, pipeline_mode=pl.Buffered(1)),
            pl.BlockSpec((None, GMLP_WIDTH, D_MODEL), layer3, pipeline_mode=pl.Buffered(1)),
            pl.BlockSpec((None, D_MODEL, D_MODEL), layer3, pipeline_mode=pl.Buffered(1)),
            pl.BlockSpec((None, 1, GMLP_WIDTH), layer3),
            pl.BlockSpec((None, 1, GMLP_WIDTH), layer3),
            pl.BlockSpec((None, GMLP_GROUPS, GMLP_CHUNK, GMLP_CHUNK), lambda i: (l, 0, 0, 0)),
            pl.BlockSpec((None, GMLP_CHUNK, GMLP_WIDTH), layer3),
            pl.BlockSpec((None, 1, D_MODEL), layer3),
            pl.BlockSpec((None, 1, D_MODEL), layer3),
        ],
        out_specs=out_specs,
        out_shape=out_shape,
        scratch_shapes=[pltpu.VMEM((tm, GMLP_WIDTH), F32)],
        compiler_params=_params(("parallel",)),
        name="post_mix",
    )(x, o, w_in_b, woa_b, wog_b, wout_b, sg, sb, ws, bs, lng, lnb)


def _rope_angles(pos):
    inv = ROPE_THETA ** (-jnp.arange(HALF_DIM, dtype=F32) / HALF_DIM)
    ang = pos.astype(F32)[:, None] * inv[None, :]
    return jnp.cos(ang), jnp.sin(ang)


def _lane_tables(cos, sin):
    reps = LANES // HEAD_DIM
    cos_t = jnp.tile(jnp.concatenate([cos, cos], -1), (1, reps))
    sin_t = jnp.tile(jnp.concatenate([-sin, sin], -1), (1, reps))
    return cos_t, sin_t


def kernel(x_prompt, x_sample, cache_k, cache_v, w_in, w_oa, w_og, w_out, lambda_q1,
           lambda_k1, lambda_q2, lambda_k2, subln_w, sgu_ln_g, sgu_ln_b, w_s, b_s, ln_g, ln_b):
    batch, seq, _ = x_prompt.shape
    dec_batch, dec_seq, _ = x_sample.shape
    assert batch == 1 and seq % ATTN_K_TILE == 0 and seq % ROW_TILE == 0
    assert (dec_batch * dec_seq) % ROW_TILE == 0 and GMLP_CHUNK % dec_seq == 0

    cos_p, sin_p = _rope_angles(jnp.arange(seq))
    cos_pl, sin_pl = _lane_tables(cos_p, sin_p)
    cos_pt, sin_pt = cos_p.T, sin_p.T
    cos_s, sin_s = _lane_tables(*_rope_angles(PAST_LEN + jnp.arange(dec_seq)))
    cos_s = jnp.tile(cos_s, (dec_batch, 1))
    sin_s = jnp.tile(sin_s, (dec_batch, 1))

    xp = x_prompt.reshape(seq, D_MODEL)
    xs = x_sample.reshape(dec_batch * dec_seq, D_MODEL)
    ckt = jnp.transpose(cache_k, (0, 1, 3, 4, 2))
    cv = cache_v.reshape(DEPTH, dec_batch, PAST_LEN * N_HEADS, VALUE_DIM)

    w_in_b = w_in.astype(BF16)
    wkt_b = jnp.transpose(w_in[:, :, ATTN_WIDTH:2 * ATTN_WIDTH], (0, 2, 1)).astype(BF16)
    woa_b = w_oa.astype(BF16)
    wog_b = w_og.astype(BF16)
    wout_b = w_out.astype(BF16)
    lam_vecs = tuple(v.reshape(DEPTH, 1, HEAD_DIM)
                     for v in (lambda_q1, lambda_k1, lambda_q2, lambda_k2))
    subw = subln_w.reshape(DEPTH, 1, VALUE_DIM)
    sg = sgu_ln_g.reshape(DEPTH, 1, GMLP_WIDTH)
    sb = sgu_ln_b.reshape(DEPTH, 1, GMLP_WIDTH)
    lng = ln_g.reshape(DEPTH, 1, D_MODEL)
    lnb = ln_b.reshape(DEPTH, 1, D_MODEL)
    bias_p = jnp.repeat(jnp.transpose(b_s, (0, 2, 1)), GMLP_WIDTH // GMLP_GROUPS, axis=2)
    reps = GMLP_CHUNK // dec_seq
    ws_s = jnp.tile(w_s[:, :, :dec_seq, :dec_seq], (1, 1, reps, reps))
    bias_s = jnp.tile(bias_p[:, :dec_seq], (1, reps, 1))

    kbuf = jnp.zeros((DEPTH, 2 * N_HEADS, HEAD_DIM, seq), F32)
    vbuf = jnp.zeros((DEPTH, seq * N_HEADS, VALUE_DIM), F32)

    ks_l, vs_l, gs_l = [], [], []
    for l in range(DEPTH):
        lam_init = 0.8 - 0.6 * math.exp(-0.3 * l)

        qb, ktb, vb, kbuf, vbuf = _qkv_prompt_call(l, xp, w_in_b, wkt_b, cos_pl, sin_pl,
                                                   cos_pt, sin_pt, kbuf, vbuf)
        op = _attn_prompt_call(l, qb, ktb, vb, lam_vecs, subw, lam_init)
        (xp,) = _post_call(l, xp, op, w_in_b, woa_b, wog_b, wout_b, sg, sb, w_s, bias_p,
                           lng, lnb, GMLP_CHUNK, False)

        qb, kb, vb, ksn, vsn = _qkv_sample_call(l, xs, w_in_b, cos_s, sin_s)
        osn = _attn_sample_call(l, qb, kb, vb, ckt, cv, lam_vecs, subw, lam_init, dec_seq)
        xs, gsn = _post_call(l, xs, osn, w_in_b, woa_b, wog_b, wout_b, sg, sb, ws_s, bias_s,
                             lng, lnb, dec_seq, True)
        ks_l.append(ksn); vs_l.append(vsn); gs_l.append(gsn)

    new_k_prompt = jnp.transpose(kbuf.reshape(DEPTH, batch, 2 * N_HEADS, HEAD_DIM, seq),
                                 (0, 1, 4, 2, 3))
    new_v_prompt = vbuf.reshape(DEPTH, batch, seq, N_HEADS, VALUE_DIM)
    new_k_sample = jnp.stack(ks_l).reshape(DEPTH, dec_batch, dec_seq, 2 * N_HEADS, HEAD_DIM)
    new_v_sample = jnp.stack(vs_l).reshape(DEPTH, dec_batch, dec_seq, N_HEADS, VALUE_DIM)
    new_gv_sample = jnp.stack(gs_l).reshape(DEPTH, dec_batch, dec_seq, GMLP_WIDTH)
    return (xp.reshape(batch, seq, D_MODEL), xs.reshape(dec_batch, dec_seq, D_MODEL),
            new_k_prompt, new_v_prompt, new_k_sample, new_v_sample, new_gv_sample)
```

```python
import functools
import math

import jax
import jax.numpy as jnp
import numpy as np
from jax import lax
from jax.experimental import pallas as pl
from jax.experimental.pallas import tpu as pltpu

D_MODEL = 1024
DEPTH = 4
PAST_LEN = 1024
CHUNK = 64
N_HEADS = 4
HEAD_DIM = 64
HALF_DIM = HEAD_DIM // 2
VALUE_DIM = 2 * HEAD_DIM
ATTN_WIDTH = N_HEADS * VALUE_DIM
GMLP_CHUNK = 128
GMLP_GROUPS = 4
GMLP_WIDTH = 512
ROPE_THETA = 10000.0
ALPHA = (2 * DEPTH) ** 0.25
LN_EPS = 1e-5
SCALE = HEAD_DIM ** -0.5
QKV_WIDTH = 3 * ATTN_WIDTH
IN_WIDTH = QKV_WIDTH + ATTN_WIDTH + 3 * GMLP_WIDTH + 2 * D_MODEL

LANES = 128
VMEM_LIMIT_BYTES = 56 * 1024 * 1024

ROW_TILE = 1024
POST_TILE = 1024
ATTN_TILE = 1024
ATTN_K_TILE = 1024
ATTN_CHAIN = 512
SAMPLE_SEQS_PER_STEP = 2
VEXT_WIDTH = 2 * VALUE_DIM

F32 = jnp.float32
BF16 = jnp.bfloat16


def _nt_dot(a, b):
    return lax.dot_general(a, b, (((1,), (1,)), ((), ())), preferred_element_type=F32)


def _dot(a, b):
    return jnp.dot(a, b, preferred_element_type=F32)


def _layernorm(x, g, b):
    mu = jnp.mean(x, axis=-1, keepdims=True)
    xc = x - mu
    var = jnp.mean(xc * xc, axis=-1, keepdims=True)
    return xc * lax.rsqrt(var + LN_EPS) * g + b


def _silu(x):
    return x * jax.nn.sigmoid(x)


def _params(semantics):
    return pltpu.CompilerParams(dimension_semantics=semantics,
                                vmem_limit_bytes=VMEM_LIMIT_BYTES)


def _rope_lanes(z, cos, sin):
    lane = lax.broadcasted_iota(jnp.int32, z.shape, 1)
    first_half = (lane % HEAD_DIM) < HALF_DIM
    partner = jnp.where(first_half, pltpu.roll(z, LANES - HALF_DIM, 1),
                        pltpu.roll(z, HALF_DIM, 1))
    return z * cos + partner * sin


def _qkv_prompt_kernel(x_ref, wq_ref, wkt_ref, wv_ref, cos_ref, sin_ref, cost_ref, sint_ref,
                       kbuf_ref, vbuf_ref, qb_ref, ktb_ref, vb_ref, kt_ref, v_ref):
    del kbuf_ref, vbuf_ref
    xb = x_ref[...].astype(BF16)
    cos = cos_ref[...]
    sin = sin_ref[...]
    q = _dot(xb, wq_ref[...])
    for c in range(ATTN_WIDTH // LANES):
        sl = slice(c * LANES, (c + 1) * LANES)
        qb_ref[:, sl] = (_rope_lanes(q[:, sl], cos, sin) * SCALE).astype(BF16)

    kt = _nt_dot(wkt_ref[...], xb)
    cost = cost_ref[...]
    sint = sint_ref[...]
    for j in range(2 * N_HEADS):
        x1 = kt[j * HEAD_DIM:j * HEAD_DIM + HALF_DIM]
        x2 = kt[j * HEAD_DIM + HALF_DIM:(j + 1) * HEAD_DIM]
        o1 = x1 * cost - x2 * sint
        o2 = x2 * cost + x1 * sint
        kt_ref[j, :HALF_DIM, :] = o1
        kt_ref[j, HALF_DIM:, :] = o2
        ktb_ref[j * HEAD_DIM:j * HEAD_DIM + HALF_DIM, :] = o1.astype(BF16)
        ktb_ref[j * HEAD_DIM + HALF_DIM:(j + 1) * HEAD_DIM, :] = o2.astype(BF16)

    v = _dot(xb, wv_ref[...])
    rows = v.shape[0]
    ones = jnp.ones((rows, VALUE_DIM), BF16)
    for h in range(N_HEADS):
        vh = v[:, h * LANES:(h + 1) * LANES]
        v_ref[pl.ds(h, rows, stride=N_HEADS), :] = vh
        vb_ref[:, h * VEXT_WIDTH:h * VEXT_WIDTH + VALUE_DIM] = vh.astype(BF16)
        vb_ref[:, h * VEXT_WIDTH + VALUE_DIM:(h + 1) * VEXT_WIDTH] = ones


def _qkv_prompt_call(l, x, w_in_b, wkt_b, cos_t, sin_t, cost_t, sint_t, kbuf, vbuf):
    n = x.shape[0]
    tm = ROW_TILE
    row = lambda i: (i, 0)
    col = lambda i: (0, i)
    any_spec = pl.BlockSpec(memory_space=pl.ANY)
    return pl.pallas_call(
        _qkv_prompt_kernel,
        grid=(n // tm,),
        in_specs=[
            pl.BlockSpec((tm, D_MODEL), row),
            pl.BlockSpec((None, D_MODEL, ATTN_WIDTH), lambda i: (l, 0, 0)),
            pl.BlockSpec((None, ATTN_WIDTH, D_MODEL), lambda i: (l, 0, 0)),
            pl.BlockSpec((None, D_MODEL, ATTN_WIDTH), lambda i: (l, 0, 2)),
            pl.BlockSpec((tm, LANES), row),
            pl.BlockSpec((tm, LANES), row),
            pl.BlockSpec((HALF_DIM, tm), col),
            pl.BlockSpec((HALF_DIM, tm), col),
            any_spec, any_spec,
        ],
        out_specs=[
            pl.BlockSpec((tm, ATTN_WIDTH), row),
            pl.BlockSpec((ATTN_WIDTH, tm), col),
            pl.BlockSpec((tm, N_HEADS * VEXT_WIDTH), row),
            pl.BlockSpec((None, 2 * N_HEADS, HEAD_DIM, tm), lambda i: (l, 0, 0, i)),
            pl.BlockSpec((None, tm * N_HEADS, VALUE_DIM), lambda i: (l, i, 0)),
        ],
        out_shape=[
            jax.ShapeDtypeStruct((n, ATTN_WIDTH), BF16),
            jax.ShapeDtypeStruct((ATTN_WIDTH, n), BF16),
            jax.ShapeDtypeStruct((n, N_HEADS * VEXT_WIDTH), BF16),
            jax.ShapeDtypeStruct(kbuf.shape, F32),
            jax.ShapeDtypeStruct(vbuf.shape, F32),
        ],
        input_output_aliases={8: 3, 9: 4},
        compiler_params=_params(("parallel",)),
        name="qkv_rope_prompt",
    )(x, w_in_b, wkt_b, w_in_b, cos_t, sin_t, cost_t, sint_t, kbuf, vbuf)


def _qkv_sample_kernel(x_ref, w_ref, cos_ref, sin_ref, qb_ref, kb_ref, vb_ref, k_ref, v_ref):
    xb = x_ref[...].astype(BF16)
    h = _dot(xb, w_ref[...])
    cos = cos_ref[...]
    sin = sin_ref[...]
    for c in range(ATTN_WIDTH // LANES):
        sl = slice(c * LANES, (c + 1) * LANES)
        q = _rope_lanes(h[:, c * LANES:(c + 1) * LANES], cos, sin)
        qb_ref[:, sl] = (q * SCALE).astype(BF16)
        k = _rope_lanes(h[:, ATTN_WIDTH + c * LANES:ATTN_WIDTH + (c + 1) * LANES], cos, sin)
        k_ref[:, sl] = k
        kb_ref[:, sl] = k.astype(BF16)
    v = h[:, 2 * ATTN_WIDTH:3 * ATTN_WIDTH]
    v_ref[...] = v
    vb_ref[...] = v.astype(BF16)


def _qkv_sample_call(l, x, w_in_b, cos_t, sin_t):
    n = x.shape[0]
    tm = ROW_TILE
    row = lambda i: (i, 0)
    return pl.pallas_call(
        _qkv_sample_kernel,
        grid=(n // tm,),
        in_specs=[
            pl.BlockSpec((tm, D_MODEL), row),
            pl.BlockSpec((None, D_MODEL, QKV_WIDTH), lambda i: (l, 0, 0)),
            pl.BlockSpec((tm, LANES), row),
            pl.BlockSpec((tm, LANES), row),
        ],
        out_specs=[pl.BlockSpec((tm, ATTN_WIDTH), row)] * 5,
        out_shape=[jax.ShapeDtypeStruct((n, ATTN_WIDTH), BF16)] * 3
        + [jax.ShapeDtypeStruct((n, ATTN_WIDTH), F32)] * 2,
        compiler_params=_params(("parallel",)),
        name="qkv_rope_sample",
    )(x, w_in_b, cos_t, sin_t)


def _lambda_value(lq1_ref, lk1_ref, lq2_ref, lk2_ref, lam_init):
    a = jnp.sum(lq1_ref[...] * lk1_ref[...], axis=-1, keepdims=True)
    b = jnp.sum(lq2_ref[...] * lk2_ref[...], axis=-1, keepdims=True)
    return jnp.exp(a) - jnp.exp(b) + lam_init


def _stack_sub_queries(q_ref, qs_ref, rows):
    lane = lax.broadcasted_iota(jnp.int32, (rows, LANES), 1)
    low = lane < HEAD_DIM
    zero = jnp.zeros((rows, LANES), BF16)
    for h in range(N_HEADS):
        qh = q_ref[:, h * LANES:(h + 1) * LANES]
        qs_ref[h, :rows, :] = jnp.where(low, qh, zero)
        qs_ref[h, rows:, :] = jnp.where(low, zero, qh)


def _combine_heads(acc0, acc1, l0, l1, lam, subw, lam_init):
    o = acc0 / l0 - lam * (acc1 / l1)
    ms = jnp.mean(o * o, axis=-1, keepdims=True)
    return o * lax.rsqrt(ms + LN_EPS) * subw * (1.0 - lam_init)


def _attn_prompt_kernel(qi_ref, ki_ref, q_ref, kt_ref, v_ref, lq1_ref, lk1_ref, lq2_ref,
                        lk2_ref, subw_ref, o_ref, qs_ref, m_ref, acc_ref, *,
                        tile, ktile, chain, lam_init):
    ratio = ktile // tile
    step = pl.program_id(0)
    qi = qi_ref[step]
    ki = ki_ref[step]
    last = qi // ratio

    @pl.when(ki == 0)
    def _():
        _stack_sub_queries(q_ref, qs_ref, tile)
        m_ref[...] = jnp.full(m_ref.shape, -jnp.inf, F32)
        acc_ref[...] = jnp.zeros(acc_ref.shape, F32)

    def chain_step(h, sub, r0, nrows, k0, k1, masked):
        width = k1 - k0
        rows = slice(sub * tile + r0, sub * tile + r0 + nrows)
        s = _dot(qs_ref[h, rows], kt_ref[h * LANES:(h + 1) * LANES, k0:k1])
        if masked:
            r = lax.broadcasted_iota(jnp.int32, (nrows, width), 0)
            c = lax.broadcasted_iota(jnp.int32, (nrows, width), 1)
            s = jnp.where((c // CHUNK) <= (r // CHUNK), s, -jnp.inf)
        m_prev = m_ref[h, rows]
        m_next = jnp.maximum(m_prev, jnp.max(s, axis=-1, keepdims=True))
        alpha = jnp.exp(m_prev - m_next)
        p = jnp.exp((s - pltpu.repeat(m_next, width // LANES, axis=1)).astype(BF16))
        m_ref[h, rows] = m_next
        pv = _dot(p, v_ref[k0:k1, h * VEXT_WIDTH:(h + 1) * VEXT_WIDTH])
        acc_ref[h, rows] = pltpu.repeat(alpha, VEXT_WIDTH // LANES, axis=1) * acc_ref[h, rows] + pv

    def update(r0, k0, k1, masked):
        for h in range(N_HEADS):
            for sub in range(2):
                chain_step(h, sub, r0, chain, k0, k1, masked)

    @pl.when(ki < last)
    def _():
        maps = [(h, sub) for h in range(N_HEADS) for sub in range(2)]
        blocks = [(h, sub, rb * chain, chain) for rb in range(tile // chain) for h, sub in maps]
        halve = lambda blk: [(blk[0], blk[1], blk[2], blk[3] // 2),
                             (blk[0], blk[1], blk[2] + blk[3] // 2, blk[3] // 2)]
        blocks = (halve(blocks[0]) + halve(blocks[1]) + blocks[2:-2]
                  + halve(blocks[-2]) + halve(blocks[-1]))
        for h, sub, r0, nrows in blocks:
            chain_step(h, sub, r0, nrows, 0, ktile, False)

    for part in range(ratio):
        @pl.when((ki == last) & (qi % ratio == part))
        def _():
            for rb in range(tile // chain):
                diag = part * tile + rb * chain
                if diag:
                    update(rb * chain, 0, diag, False)
                update(rb * chain, diag, diag + chain, True)
            lam = _lambda_value(lq1_ref, lk1_ref, lq2_ref, lk2_ref, lam_init)
            subw = subw_ref[...]
            for h in range(N_HEADS):
                acc = acc_ref[h]
                o_ref[:, h * LANES:(h + 1) * LANES] = _combine_heads(
                    acc[:tile, :VALUE_DIM], acc[tile:, :VALUE_DIM],
                    acc[:tile, VALUE_DIM:], acc[tile:, VALUE_DIM:], lam, subw, lam_init)


def _attn_prompt_call(l, qb, ktb, vb, lam_vecs, subw, lam_init):
    s = qb.shape[0]
    t, kt = ATTN_TILE, ATTN_K_TILE
    ratio = kt // t
    nq = s // t
    qi = np.concatenate([np.full(i // ratio + 1, i, np.int32) for i in range(nq)])
    ki = np.concatenate([np.arange(i // ratio + 1, dtype=np.int32) for i in range(nq)])
    vec = pl.BlockSpec((None, 1, HEAD_DIM), lambda g, qi, ki: (l, 0, 0))
    grid_spec = pltpu.PrefetchScalarGridSpec(
        num_scalar_prefetch=2,
        grid=(len(qi),),
        in_specs=[
            pl.BlockSpec((t, ATTN_WIDTH), lambda g, qi, ki: (qi[g], 0)),
            pl.BlockSpec((ATTN_WIDTH, kt), lambda g, qi, ki: (0, ki[g])),
            pl.BlockSpec((kt, N_HEADS * VEXT_WIDTH), lambda g, qi, ki: (ki[g], 0)),
            vec, vec, vec, vec,
            pl.BlockSpec((None, 1, VALUE_DIM), lambda g, qi, ki: (l, 0, 0)),
        ],
        out_specs=pl.BlockSpec((t, ATTN_WIDTH), lambda g, qi, ki: (qi[g], 0)),
        scratch_shapes=[
            pltpu.VMEM((N_HEADS, 2 * t, LANES), BF16),
            pltpu.VMEM((N_HEADS, 2 * t, LANES), F32),
            pltpu.VMEM((N_HEADS, 2 * t, VEXT_WIDTH), F32),
        ],
    )
    return pl.pallas_call(
        functools.partial(_attn_prompt_kernel, tile=t, ktile=kt, chain=ATTN_CHAIN,
                          lam_init=lam_init),
        grid_spec=grid_spec,
        out_shape=jax.ShapeDtypeStruct((s, ATTN_WIDTH), F32),
        compiler_params=_params(("arbitrary",)),
        name="diff_attn_prompt",
    )(jnp.asarray(qi), jnp.asarray(ki), qb, ktb, vb, *lam_vecs, subw)


def _attn_sample_kernel(q_ref, k_ref, v_ref, ckt_ref, cv_ref, lq1_ref, lk1_ref, lq2_ref,
                        lk2_ref, subw_ref, o_ref, qs_ref, *, rows, nseq, lam_init):
    lam = _lambda_value(lq1_ref, lk1_ref, lq2_ref, lk2_ref, lam_init)
    subw = subw_ref[...]
    past = ckt_ref.shape[-1]
    for sq in range(nseq):
        rs = pl.ds(sq * rows, rows)
        _stack_sub_queries(q_ref.at[rs], qs_ref.at[sq], rows)
    for sq in range(nseq):
        rs = slice(sq * rows, (sq + 1) * rows)
        for h in range(N_HEADS):
            hs = slice(h * LANES, (h + 1) * LANES)
            qs = qs_ref[sq, h]
            ckt = ckt_ref[sq, 2 * h:2 * h + 2].reshape(LANES, past).astype(BF16)
            s_c = _dot(qs, ckt)
            s_n = _nt_dot(qs, k_ref[rs, hs])
            m = jnp.maximum(jnp.max(s_c, axis=-1, keepdims=True),
                            jnp.max(s_n, axis=-1, keepdims=True))
            p_c = jnp.exp(s_c - m)
            p_n = jnp.exp(s_n - m)
            l = jnp.sum(p_c, axis=-1, keepdims=True) + jnp.sum(p_n, axis=-1, keepdims=True)
            cv_h = cv_ref[sq, pl.ds(h, past, stride=N_HEADS), :]
            acc = _dot(p_c.astype(BF16), cv_h.astype(BF16))
            acc = acc + _dot(p_n.astype(BF16), v_ref[rs, hs])
            o_ref[rs, hs] = _combine_heads(acc[:rows], acc[rows:], l[:rows], l[rows:], lam,
                                           subw, lam_init)


def _attn_sample_call(l, qb, kb, vb, ckt, cv, lam_vecs, subw, lam_init, rows):
    n = qb.shape[0] // rows
    nseq = SAMPLE_SEQS_PER_STEP
    past = cv.shape[2] // N_HEADS
    vec = pl.BlockSpec((None, 1, HEAD_DIM), lambda i: (l, 0, 0))
    new = pl.BlockSpec((nseq * rows, ATTN_WIDTH), lambda i: (i, 0))
    return pl.pallas_call(
        functools.partial(_attn_sample_kernel, rows=rows, nseq=nseq, lam_init=lam_init),
        grid=(n // nseq,),
        in_specs=[
            new, new, new,
            pl.BlockSpec((None, nseq, 2 * N_HEADS, HEAD_DIM, past), lambda i: (l, i, 0, 0, 0)),
            pl.BlockSpec((None, nseq, past * N_HEADS, VALUE_DIM), lambda i: (l, i, 0, 0)),
            vec, vec, vec, vec,
            pl.BlockSpec((None, 1, VALUE_DIM), lambda i: (l, 0, 0)),
        ],
        out_specs=new,
        out_shape=jax.ShapeDtypeStruct((n * rows, ATTN_WIDTH), F32),
        scratch_shapes=[pltpu.VMEM((nseq, N_HEADS, 2 * rows, LANES), BF16)],
        compiler_params=_params(("parallel",)),
        name="diff_attn_sample",
    )(qb, kb, vb, ckt, cv, *lam_vecs, subw)


_GA = (QKV_WIDTH, QKV_WIDTH + ATTN_WIDTH)
_U = (_GA[1], _GA[1] + GMLP_WIDTH)
_VG = (_U[1], _U[1] + GMLP_WIDTH)
_GG = (_VG[1], _VG[1] + GMLP_WIDTH)
_MA = (_GG[1], _GG[1] + D_MODEL)
_MB = (_MA[1], IN_WIDTH)


def _post_kernel(x_ref, o_ref, w_ref, woa_ref, wog_ref, wout_ref, sg_ref, sb_ref, ws_ref,
                 bs_ref, lng_ref, lnb_ref, *out_and_scratch, period, emit_vn):
    if emit_vn:
        xo_ref, vn_ref, sgu_ref = out_and_scratch
    else:
        xo_ref, sgu_ref = out_and_scratch
    tm = x_ref.shape[0]
    x = x_ref[...]
    xb = x.astype(BF16)

    def proj(cols):
        return _dot(xb, w_ref[:, cols[0]:cols[1]])

    ta = (o_ref[...] * _silu(proj(_GA))).astype(BF16)
    ya = _dot(ta, woa_ref[...])

    vn = _layernorm(proj(_VG), sg_ref[...], sb_ref[...])
    if emit_vn:
        vn_ref[...] = vn
    vnb = vn.astype(BF16)
    r = lax.broadcasted_iota(jnp.int32, (GMLP_CHUNK, GMLP_CHUNK), 0)
    c = lax.broadcasted_iota(jnp.int32, (GMLP_CHUNK, GMLP_CHUNK), 1)
    causal = (c <= r) & ((r // period) == (c // period))
    bias = bs_ref[...]
    for g in range(GMLP_GROUPS):
        gs = slice(g * LANES, (g + 1) * LANES)
        wsg = jnp.where(causal, ws_ref[g], 0.0).astype(BF16)
        for ch in range(tm // GMLP_CHUNK):
            rs = slice(ch * GMLP_CHUNK, (ch + 1) * GMLP_CHUNK)
            sgu_ref[rs, gs] = _dot(wsg, vnb[rs, gs]) + bias[:, gs]
    tg = (proj(_U) * sgu_ref[...] * _silu(proj(_GG))).astype(BF16)
    yg = _dot(tg, wog_ref[...])

    merged = jax.nn.sigmoid(proj(_MA)) * ya + jax.nn.sigmoid(proj(_MB)) * yg
    y = _dot(merged.astype(BF16), wout_ref[...])
    xo_ref[...] = _layernorm(ALPHA * x + y, lng_ref[...], lnb_ref[...])


def _post_call(l, x, o, w_in_b, woa_b, wog_b, wout_b, sg, sb, ws, bs, lng, lnb, period,
               emit_vn):
    n = x.shape[0]
    tm = POST_TILE
    row = lambda i: (i, 0)
    layer3 = lambda i: (l, 0, 0)
    out_specs = [pl.BlockSpec((tm, D_MODEL), row)]
    out_shape = [jax.ShapeDtypeStruct((n, D_MODEL), F32)]
    if emit_vn:
        out_specs.append(pl.BlockSpec((tm, GMLP_WIDTH), row))
        out_shape.append(jax.ShapeDtypeStruct((n, GMLP_WIDTH), F32))
    return pl.pallas_call(
        functools.partial(_post_kernel, period=period, emit_vn=emit_vn),
        grid=(n // tm,),
        in_specs=[
            pl.BlockSpec((tm, D_MODEL), row),
            pl.BlockSpec((tm, ATTN_WIDTH), row),
            pl.BlockSpec((None, D_MODEL, IN_WIDTH), layer3, pipeline_mode=pl.Buffered(1)),
            pl.BlockSpec((None, ATTN_WIDTH, D_MODEL), layer3, pipeline_mode=pl.Buffered(1)),
            pl.BlockSpec((None, GMLP_WIDTH, D_MODEL), layer3, pipeline_mode=pl.Buffered(1)),
            pl.BlockSpec((None, D_MODEL, D_MODEL), layer3, pipeline_mode=pl.Buffered(1)),
            pl.BlockSpec((None, 1, GMLP_WIDTH), layer3),
            pl.BlockSpec((None, 1, GMLP_WIDTH), layer3),
            pl.BlockSpec((None, GMLP_GROUPS, GMLP_CHUNK, GMLP_CHUNK), lambda i: (l, 0, 0, 0)),
            pl.BlockSpec((None, GMLP_CHUNK, GMLP_WIDTH), layer3),
            pl.BlockSpec((None, 1, D_MODEL), layer3),
            pl.BlockSpec((None, 1, D_MODEL), layer3),
        ],
        out_specs=out_specs,
        out_shape=out_shape,
        scratch_shapes=[pltpu.VMEM((tm, GMLP_WIDTH), F32)],
        compiler_params=_params(("parallel",)),
        name="post_mix",
    )(x, o, w_in_b, woa_b, wog_b, wout_b, sg, sb, ws, bs, lng, lnb)


def _rope_angles(pos):
    inv = ROPE_THETA ** (-jnp.arange(HALF_DIM, dtype=F32) / HALF_DIM)
    ang = pos.astype(F32)[:, None] * inv[None, :]
    return jnp.cos(ang), jnp.sin(ang)


def _lane_tables(cos, sin):
    reps = LANES // HEAD_DIM
    cos_t = jnp.tile(jnp.concatenate([cos, cos], -1), (1, reps))
    sin_t = jnp.tile(jnp.concatenate([-sin, sin], -1), (1, reps))
    return cos_t, sin_t


def kernel(x_prompt, x_sample, cache_k, cache_v, w_in, w_oa, w_og, w_out, lambda_q1,
           lambda_k1, lambda_q2, lambda_k2, subln_w, sgu_ln_g, sgu_ln_b, w_s, b_s, ln_g, ln_b):
    batch, seq, _ = x_prompt.shape
    dec_batch, dec_seq, _ = x_sample.shape
    assert batch == 1 and seq % ATTN_K_TILE == 0 and seq % ROW_TILE == 0
    assert (dec_batch * dec_seq) % ROW_TILE == 0 and GMLP_CHUNK % dec_seq == 0

    cos_p, sin_p = _rope_angles(jnp.arange(seq))
    cos_pl, sin_pl = _lane_tables(cos_p, sin_p)
    cos_pt, sin_pt = cos_p.T, sin_p.T
    cos_s, sin_s = _lane_tables(*_rope_angles(PAST_LEN + jnp.arange(dec_seq)))
    cos_s = jnp.tile(cos_s, (dec_batch, 1))
    sin_s = jnp.tile(sin_s, (dec_batch, 1))

    xp = x_prompt.reshape(seq, D_MODEL)
    xs = x_sample.reshape(dec_batch * dec_seq, D_MODEL)
    ckt = jnp.transpose(cache_k, (0, 1, 3, 4, 2))
    cv = cache_v.reshape(DEPTH, dec_batch, PAST_LEN * N_HEADS, VALUE_DIM)

    w_in_b = w_in.astype(BF16)
    wkt_b = jnp.transpose(w_in[:, :, ATTN_WIDTH:2 * ATTN_WIDTH], (0, 2, 1)).astype(BF16)
    woa_b = w_oa.astype(BF16)
    wog_b = w_og.astype(BF16)
    wout_b = w_out.astype(BF16)
    lam_vecs = tuple(v.reshape(DEPTH, 1, HEAD_DIM)
                     for v in (lambda_q1, lambda_k1, lambda_q2, lambda_k2))
    subw = subln_w.reshape(DEPTH, 1, VALUE_DIM)
    sg = sgu_ln_g.reshape(DEPTH, 1, GMLP_WIDTH)
    sb = sgu_ln_b.reshape(DEPTH, 1, GMLP_WIDTH)
    lng = ln_g.reshape(DEPTH, 1, D_MODEL)
    lnb = ln_b.reshape(DEPTH, 1, D_MODEL)
    bias_p = jnp.repeat(jnp.transpose(b_s, (0, 2, 1)), GMLP_WIDTH // GMLP_GROUPS, axis=2)
    reps = GMLP_CHUNK // dec_seq
    ws_s = jnp.tile(w_s[:, :, :dec_seq, :dec_seq], (1, 1, reps, reps))
    bias_s = jnp.tile(bias_p[:, :dec_seq], (1, reps, 1))

    kbuf = jnp.zeros((DEPTH, 2 * N_HEADS, HEAD_DIM, seq), F32)
    vbuf = jnp.zeros((DEPTH, seq * N_HEADS, VALUE_DIM), F32)

    ks_l, vs_l, gs_l = [], [], []
    for l in range(DEPTH):
        lam_init = 0.8 - 0.6 * math.exp(-0.3 * l)

        qb, ktb, vb, kbuf, vbuf = _qkv_prompt_call(l, xp, w_in_b, wkt_b, cos_pl, sin_pl,
                                                   cos_pt, sin_pt, kbuf, vbuf)
        op = _attn_prompt_call(l, qb, ktb, vb, lam_vecs, subw, lam_init)
        (xp,) = _post_call(l, xp, op, w_in_b, woa_b, wog_b, wout_b, sg, sb, w_s, bias_p,
                           lng, lnb, GMLP_CHUNK, False)

        qb, kb, vb, ksn, vsn = _qkv_sample_call(l, xs, w_in_b, cos_s, sin_s)
        osn = _attn_sample_call(l, qb, kb, vb, ckt, cv, lam_vecs, subw, lam_init, dec_seq)
        xs, gsn = _post_call(l, xs, osn, w_in_b, woa_b, wog_b, wout_b, sg, sb, ws_s, bias_s,
                             lng, lnb, dec_seq, True)
        ks_l.append(ksn); vs_l.append(vsn); gs_l.append(gsn)

    new_k_prompt = jnp.transpose(kbuf.reshape(DEPTH, batch, 2 * N_HEADS, HEAD_DIM, seq),
                                 (0, 1, 4, 2, 3))
    new_v_prompt = vbuf.reshape(DEPTH, batch, seq, N_HEADS, VALUE_DIM)
    new_k_sample = jnp.stack(ks_l).reshape(DEPTH, dec_batch, dec_seq, 2 * N_HEADS, HEAD_DIM)
    new_v_sample = jnp.stack(vs_l).reshape(DEPTH, dec_batch, dec_seq, N_HEADS, VALUE_DIM)
    new_gv_sample = jnp.stack(gs_l).reshape(DEPTH, dec_batch, dec_seq, GMLP_WIDTH)
    return (xp.reshape(batch, seq, D_MODEL), xs.reshape(dec_batch, dec_seq, D_MODEL),
            new_k_prompt, new_v_prompt, new_k_sample, new_v_sample, new_gv_sample)
```
